```python
import jax, jax.numpy as jnp
from jax import lax
import numpy as np

D_MODEL = 1024
BATCH = 8
SEQ = 2048
DEPTH = 2
DEC_BATCH = 128
DEC_SEQ = 1
PAST_LEN = 16384
PAGE_SIZE = 128

DN_HEADS = 8
DN_DK = 128
DN_DV = 128
DN_KEY = DN_HEADS * DN_DK
DN_VAL = DN_HEADS * DN_DV
QKV_DIM = 2 * DN_KEY + DN_VAL
SHORT_CONV = 4
CHUNK = 64
CONF_DIM = D_MODEL
CONF_CONV = 31
N_EXPERTS = 32
TOP_K = 4
D_FF = D_MODEL
SWIGLU_LIMIT = 7.0
SWIGLU_ALPHA = 1.702
MOE_BLOCK = 64
NORM_EPS = 1e-6
IN_SPLITS = (QKV_DIM, QKV_DIM + DN_VAL, QKV_DIM + DN_VAL + DN_HEADS, QKV_DIM + DN_VAL + 2 * DN_HEADS,
             QKV_DIM + DN_VAL + 2 * DN_HEADS + 2 * CONF_DIM)
IN_DIM = QKV_DIM + DN_VAL + 2 * DN_HEADS + 2 * CONF_DIM + 2 * D_MODEL

kernel_name = 'hybrid_gdn_conformer_moe_decoder_step'


def rmsnorm(x, g):
    xf = x.astype(jnp.float32)
    y = xf * lax.rsqrt(jnp.mean(xf * xf, axis=-1, keepdims=True) + NORM_EPS)
    return (y * g.astype(jnp.float32)).astype(x.dtype)


def layernorm(x, g, b):
    xf = x.astype(jnp.float32)
    mu = jnp.mean(xf, axis=-1, keepdims=True)
    xc = xf - mu
    y = xc * lax.rsqrt(jnp.mean(xc * xc, axis=-1, keepdims=True) + NORM_EPS)
    return (y * g.astype(jnp.float32) + b.astype(jnp.float32)).astype(x.dtype)


def l2norm(x):
    return x * lax.rsqrt(jnp.sum(x * x, axis=-1, keepdims=True) + NORM_EPS)


def causal_dwconv(x_ext, w):
    return lax.conv_general_dilated(x_ext, w[:, None, :].astype(x_ext.dtype), window_strides=(1,),
                                    padding='VALID', dimension_numbers=('NWC', 'WIO', 'NWC'),
                                    feature_group_count=x_ext.shape[-1])


def gated_delta_chunked(q, k, v, g, beta, s0):
    B, T, H, _ = q.shape
    n = T // CHUNK

    def chunks(a):
        a = a.reshape((B, n, CHUNK, H) + a.shape[3:])
        return jnp.moveaxis(a, (1, 3), (0, 2))

    qc, kc, vc, bc = chunks(q), chunks(k), chunks(v), chunks(beta)
    gc = jnp.cumsum(chunks(g), axis=-1)
    idx = jnp.arange(CHUNK)
    causal = idx[:, None] >= idx[None, :]
    decay = jnp.exp(jnp.where(causal, gc[..., :, None] - gc[..., None, :], -jnp.inf))
    kk = jnp.einsum('nbhid,nbhjd->nbhij', kc * bc[..., None], kc) * decay
    a_mat = jnp.where(idx[:, None] > idx[None, :], kk, 0.0) + jnp.eye(CHUNK, dtype=jnp.float32)
    rhs = jnp.concatenate([vc * bc[..., None], kc * (bc * jnp.exp(gc))[..., None]], axis=-1)
    sol = lax.linalg.triangular_solve(a_mat, rhs, left_side=True, lower=True, unit_diagonal=True)
    u, w = sol[..., :DN_DV], sol[..., DN_DV:]

    def step(S, inp):
        qi, ki, ui, wi, gi, di = inp
        v_new = ui - jnp.einsum('bhck,bhkv->bhcv', wi, S)
        intra = jnp.einsum('bhik,bhjk->bhij', qi, ki) * di
        o = (jnp.einsum('bhck,bhkv->bhcv', qi * jnp.exp(gi)[..., None], S)
             + jnp.einsum('bhij,bhjv->bhiv', intra, v_new))
        g_last = gi[..., -1:]
        S = (S * jnp.exp(g_last)[..., None]
             + jnp.einsum('bhck,bhcv->bhkv', ki * jnp.exp(g_last - gi)[..., None], v_new))
        return S, o

    s_fin, oc = lax.scan(step, s0, (qc, kc, u, w, gc, decay))
    o = jnp.moveaxis(oc, (0, 2), (1, 3)).reshape(B, T, H, DN_DV)
    return o, s_fin


def gated_delta_recurrent(q, k, v, g, beta, s0):
    def step(S, inp):
        qt, kt, vt, gt, bt = inp
        S = S * jnp.exp(gt)[..., None, None]
        delta = (vt - jnp.einsum('bhk,bhkv->bhv', kt, S)) * bt[..., None]
        S = S + kt[..., :, None] * delta[..., None, :]
        return S, jnp.einsum('bhk,bhkv->bhv', qt, S)

    xs = (jnp.moveaxis(q, 1, 0), jnp.moveaxis(k, 1, 0), jnp.moveaxis(v, 1, 0),
          jnp.moveaxis(g, 1, 0), jnp.moveaxis(beta, 1, 0))
    s_fin, o = lax.scan(step, s0, xs)
    return jnp.moveaxis(o, 0, 1), s_fin


def token_mixers(h, s_dn, buf_qkv, buf_glu, w_in, w_qkv_conv, a_log, dt_bias, g_dn_norm,
                 w_dw_conv, b_dw_conv, g_ln_conv, b_ln_conv, w_pw_conv, b_pw_conv, w_out, chunked):
    B, T, _ = h.shape
    f32 = jnp.float32
    proj = h @ w_in
    qkv_pre, z, b_logit, a_logit, glu, mgate = jnp.split(proj, IN_SPLITS, axis=-1)
    qkv_ext = jnp.concatenate([buf_qkv.astype(qkv_pre.dtype), qkv_pre], axis=1)
    new_buf_qkv = qkv_ext[:, -(SHORT_CONV - 1):]
    qkv = jax.nn.silu(causal_dwconv(qkv_ext, w_qkv_conv)).astype(f32)
    q, k, v = jnp.split(qkv, (DN_KEY, 2 * DN_KEY), axis=-1)
    q = l2norm(q.reshape(B, T, DN_HEADS, DN_DK)) * (DN_DK ** -0.5)
    k = l2norm(k.reshape(B, T, DN_HEADS, DN_DK))
    v = v.reshape(B, T, DN_HEADS, DN_DV)
    beta = jax.nn.sigmoid(b_logit.astype(f32))
    g = -jnp.exp(a_log.astype(f32)) * jax.nn.softplus(a_logit.astype(f32) + dt_bias.astype(f32))
    s0 = s_dn.astype(f32)
    if chunked:
        o, s_new = gated_delta_chunked(q, k, v, g, beta, s0)
    else:
        o, s_new = gated_delta_recurrent(q, k, v, g, beta, s0)
    zf = z.astype(f32).reshape(B, T, DN_HEADS, DN_DV)
    o = (o * lax.rsqrt(jnp.mean(o * o, axis=-1, keepdims=True) + NORM_EPS)
         * g_dn_norm.astype(f32) * jax.nn.silu(zf))
    dn_out = o.reshape(B, T, DN_VAL).astype(h.dtype)
    u_val, u_gate = jnp.split(glu, 2, axis=-1)
    u = u_val * jax.nn.sigmoid(u_gate)
    u_ext = jnp.concatenate([buf_glu.astype(u.dtype), u], axis=1)
    new_buf_glu = u_ext[:, -(CONF_CONV - 1):]
    cv = causal_dwconv(u_ext, w_dw_conv) + b_dw_conv
    cv = jax.nn.silu(layernorm(cv, g_ln_conv, b_ln_conv))
    cv_out = cv @ w_pw_conv + b_pw_conv
    gate_dn, gate_cv = jnp.split(jax.nn.sigmoid(mgate), 2, axis=-1)
    merged = gate_dn * dn_out + gate_cv * cv_out
    return merged @ w_out, s_new, new_buf_qkv, new_buf_glu


def moe(h, w_router, b_router, w_gate_up, b_gate_up, w_down, b_down):
    T, D = h.shape
    logits = h.astype(jnp.float32) @ w_router.astype(jnp.float32) + b_router.astype(jnp.float32)
    top_val, top_idx = lax.top_k(logits, TOP_K)
    gates = jax.nn.softmax(top_val, axis=-1).astype(h.dtype)
    M = T * TOP_K
    n_blocks = -(-M // MOE_BLOCK) + N_EXPERTS
    flat_e = top_idx.reshape(M)
    flat_tok = jnp.arange(M, dtype=jnp.int32) // TOP_K
    order = jnp.argsort(flat_e)
    sorted_e = flat_e[order]
    counts = jnp.bincount(flat_e, length=N_EXPERTS)
    blocks_per_e = (counts + MOE_BLOCK - 1) // MOE_BLOCK
    cum_blocks = jnp.cumsum(blocks_per_e)
    pad_start = (cum_blocks - blocks_per_e) * MOE_BLOCK
    grp_start = jnp.cumsum(counts) - counts
    dest = pad_start[sorted_e] + jnp.arange(M, dtype=jnp.int32) - grp_start[sorted_e]
    row_tok = jnp.full((n_blocks * MOE_BLOCK,), T, jnp.int32).at[dest].set(flat_tok[order])
    x_pad = jnp.concatenate([h, jnp.zeros((1, D), h.dtype)], axis=0)[row_tok]
    x_pad = x_pad.reshape(n_blocks, MOE_BLOCK, D)
    block_e = jnp.minimum(jnp.searchsorted(cum_blocks, jnp.arange(n_blocks), side='right'), N_EXPERTS - 1)

    def expert_block(args):
        xb, e = args
        gu = xb @ w_gate_up[e] + b_gate_up[e]
        gl, up = jnp.split(gu, 2, axis=-1)
        gl = jnp.minimum(gl, SWIGLU_LIMIT)
        up = jnp.clip(up, -SWIGLU_LIMIT, SWIGLU_LIMIT)
        act = gl * jax.nn.sigmoid(SWIGLU_ALPHA * gl) * (up + 1.0)
        return act @ w_down[e] + b_down[e]

    y_pad = lax.map(expert_block, (x_pad, block_e)).reshape(n_blocks * MOE_BLOCK, D)
    slot_pos = jnp.zeros((M,), jnp.int32).at[order].set(dest)
    y = y_pad[slot_pos].reshape(T, TOP_K, D)
    return jnp.einsum('tkd,tk->td', y, gates)


def trunk(x, c, s_dn, buf_qkv, buf_glu, chunked, p):
    new_s, new_q, new_g = [], [], []
    for l in range(DEPTH):
        ada = jax.nn.silu(c) @ p['w_ada'][l] + p['b_ada'][l]
        sh1, sc1, gt1, sh2, sc2, gt2 = jnp.split(ada[:, None, :], 6, axis=-1)
        h = rmsnorm(x, p['g_pre_mix'][l]) * (1.0 + sc1) + sh1
        m, s, bq, bg = token_mixers(h, s_dn[l], buf_qkv[l], buf_glu[l], p['w_in'][l], p['w_qkv_conv'][l],
                                    p['a_log'][l], p['dt_bias'][l], p['g_dn_norm'][l], p['w_dw_conv'][l],
                                    p['b_dw_conv'][l], p['g_ln_conv'][l], p['b_ln_conv'][l],
                                    p['w_pw_conv'][l], p['b_pw_conv'][l], p['w_out'][l], chunked)
        x = x + gt1 * rmsnorm(m, p['g_post_mix'][l])
        h = rmsnorm(x, p['g_pre_ffn'][l]) * (1.0 + sc2) + sh2
        B, T, D = h.shape
        f = moe(h.reshape(B * T, D), p['w_router'][l], p['b_router'][l], p['w_gate_up'][l],
                p['b_gate_up'][l], p['w_down'][l], p['b_down'][l]).reshape(B, T, D)
        x = x + gt2 * rmsnorm(f, p['g_post_ffn'][l])
        new_s.append(s)
        new_q.append(bq)
        new_g.append(bg)
    return x, jnp.stack(new_s), jnp.stack(new_q), jnp.stack(new_g)


def setup_inputs(seed: int = 0) -> dict:
    key = jax.random.key(seed)
    ks = iter(jax.random.split(key, 40))

    def nrm(shape, s):
        return jax.random.normal(next(ks), shape, jnp.float32) * s

    def gain(shape):
        return 1.0 + nrm(shape, 0.05)

    D = D_MODEL
    return {
        'x_prompt': nrm((BATCH, SEQ, D), 1.0),
        'x_sample': nrm((DEC_BATCH, DEC_SEQ, D), 1.0),
        'c_prompt': nrm((BATCH, D), 1.0),
        'c_sample': nrm((DEC_BATCH, D), 1.0),
        'state_dn': nrm((DEPTH, DEC_BATCH, DN_HEADS, DN_DK, DN_DV), 0.1),
        'state_qkv_conv': nrm((DEPTH, DEC_BATCH, SHORT_CONV - 1, QKV_DIM), 1.0),
        'state_glu_conv': nrm((DEPTH, DEC_BATCH, CONF_CONV - 1, CONF_DIM), 0.5),
        'w_ada': nrm((DEPTH, D, 6 * D), 0.5 * D ** -0.5),
        'b_ada': nrm((DEPTH, 6 * D), 0.02),
        'g_pre_mix': gain((DEPTH, D)),
        'g_post_mix': gain((DEPTH, D)),
        'g_pre_ffn': gain((DEPTH, D)),
        'g_post_ffn': gain((DEPTH, D)),
        'w_in': nrm((DEPTH, D, IN_DIM), D ** -0.5),
        'w_qkv_conv': nrm((DEPTH, SHORT_CONV, QKV_DIM), SHORT_CONV ** -0.5),
        'a_log': jnp.log(jax.random.uniform(next(ks), (DEPTH, DN_HEADS), jnp.float32, 1.0, 16.0)),
        'dt_bias': nrm((DEPTH, DN_HEADS), 0.1),
        'g_dn_norm': gain((DEPTH, DN_DV)),
        'w_dw_conv': nrm((DEPTH, CONF_CONV, CONF_DIM), CONF_CONV ** -0.5),
        'b_dw_conv': nrm((DEPTH, CONF_DIM), 0.02),
        'g_ln_conv': gain((DEPTH, CONF_DIM)),
        'b_ln_conv': nrm((DEPTH, CONF_DIM), 0.02),
        'w_pw_conv': nrm((DEPTH, CONF_DIM, D), CONF_DIM ** -0.5),
        'b_pw_conv': nrm((DEPTH, D), 0.02),
        'w_out': nrm((DEPTH, D, D), D ** -0.5),
        'w_router': nrm((DEPTH, D, N_EXPERTS), D ** -0.5),
        'b_router': nrm((DEPTH, N_EXPERTS), 0.01),
        'w_gate_up': nrm((DEPTH, N_EXPERTS, D, 2 * D_FF), D ** -0.5),
        'b_gate_up': nrm((DEPTH, N_EXPERTS, 2 * D_FF), 0.02),
        'w_down': nrm((DEPTH, N_EXPERTS, D_FF, D), D_FF ** -0.5),
        'b_down': nrm((DEPTH, N_EXPERTS, D), 0.02),
    }


def reference(x_prompt, x_sample, c_prompt, c_sample, state_dn, state_qkv_conv, state_glu_conv,
              w_ada, b_ada, g_pre_mix, g_post_mix, g_pre_ffn, g_post_ffn, w_in, w_qkv_conv, a_log,
              dt_bias, g_dn_norm, w_dw_conv, b_dw_conv, g_ln_conv, b_ln_conv, w_pw_conv, b_pw_conv,
              w_out, w_router, b_router, w_gate_up, b_gate_up, w_down, b_down):
    p = dict(w_ada=w_ada, b_ada=b_ada, g_pre_mix=g_pre_mix, g_post_mix=g_post_mix, g_pre_ffn=g_pre_ffn,
             g_post_ffn=g_post_ffn, w_in=w_in, w_qkv_conv=w_qkv_conv, a_log=a_log, dt_bias=dt_bias,
             g_dn_norm=g_dn_norm, w_dw_conv=w_dw_conv, b_dw_conv=b_dw_conv, g_ln_conv=g_ln_conv,
             b_ln_conv=b_ln_conv, w_pw_conv=w_pw_conv, b_pw_conv=b_pw_conv, w_out=w_out,
             w_router=w_router, b_router=b_router, w_gate_up=w_gate_up, b_gate_up=b_gate_up,
             w_down=w_down, b_down=b_down)
    bp = x_prompt.shape[0]
    s0_p = jnp.zeros((DEPTH, bp, DN_HEADS, DN_DK, DN_DV), jnp.float32)
    bq0_p = jnp.zeros((DEPTH, bp, SHORT_CONV - 1, QKV_DIM), x_prompt.dtype)
    bg0_p = jnp.zeros((DEPTH, bp, CONF_CONV - 1, CONF_DIM), x_prompt.dtype)
    y_prompt, s_p, q_p, g_p = trunk(x_prompt, c_prompt, s0_p, bq0_p, bg0_p, True, p)
    y_sample, s_s, q_s, g_s = trunk(x_sample, c_sample, state_dn, state_qkv_conv, state_glu_conv, False, p)
    return (y_prompt, y_sample,
            s_p.astype(state_dn.dtype), q_p.astype(state_qkv_conv.dtype), g_p.astype(state_glu_conv.dtype),
            s_s.astype(state_dn.dtype), q_s.astype(state_qkv_conv.dtype), g_s.astype(state_glu_conv.dtype))
```

```python
import functools

import jax
import jax.numpy as jnp
from jax import lax
from jax.experimental import pallas as pl
from jax.experimental.pallas import tpu as pltpu

F32, BF16, I32 = jnp.float32, jnp.bfloat16, jnp.int32

NORM_EPS = 1e-6
HEAD_DIM = 128
TOP_K = 4
SWIGLU_LIMIT = 7.0
SWIGLU_ALPHA = 1.702

LANES = 128
SUBLANES = 8
VMEM_LIMIT_BYTES = 56 * 1024 * 1024

DELTA_CHUNK = 64
CONV_ROWS = 128
MM_ROWS = 1024
MM_COLS = 512
ROW_TILE = 512
MOE_ROWS = 256
COMBINE_TOKENS = 64
DECODE_SEQS = 8

NEG_BIG = -1e30


def _params(*sem):
    return pltpu.CompilerParams(dimension_semantics=sem, vmem_limit_bytes=VMEM_LIMIT_BYTES)


def _sigmoid(x):
    return 1.0 / (1.0 + jnp.exp(-x))


def _silu(x):
    return x * _sigmoid(x)


def _softplus(x):
    return jnp.maximum(x, 0.0) + jnp.log(1.0 + jnp.exp(-jnp.abs(x)))


def _rms(x, g):
    return x * lax.rsqrt(jnp.mean(x * x, axis=-1, keepdims=True) + NORM_EPS) * g


def _bdot(a, b):
    return jnp.dot(a.astype(BF16), b.astype(BF16), preferred_element_type=F32)


def _bdot_nt(a, b):
    return lax.dot_general(a.astype(BF16), b.astype(BF16), (((1,), (1,)), ((), ())),
                           preferred_element_type=F32)


def _fdot(a, b):
    return jnp.dot(a, b, preferred_element_type=F32, precision=lax.Precision.HIGHEST)


def _tile(n, t):
    t = min(n, t)
    assert n % t == 0, (n, t)
    return t


def _mod_spec(per_row, tr, d):
    return pl.BlockSpec((1, tr if per_row else 1, d), (lambda g, r: (g, r, 0)) if per_row else (lambda g, r: (g, 0, 0)))


def _mm_kernel(*refs, in_silu, act, has_bias, glu):
    x_ref, w_ref = refs[0], refs[1]
    pos = 2
    w2_ref = None
    if glu:
        w2_ref = refs[pos]
        pos += 1
    b_ref = None
    if has_bias:
        b_ref = refs[pos]
        pos += 1
    o_ref = refs[pos]
    x = x_ref[...]
    if in_silu:
        x = _silu(x.astype(F32))
    x = x.astype(BF16)
    y = _bdot(x, w_ref[...])
    if has_bias:
        y = y + b_ref[...]
    if glu:
        y = y * _sigmoid(_bdot(x, w2_ref[...]))
    if act == "silu":
        y = _silu(y)
    elif act == "sigmoid":
        y = _sigmoid(y)
    o_ref[...] = y.astype(o_ref.dtype)


def _mm(x, w, out_dtype, *, b=None, w2=None, act=None, in_silu=False, name="mm"):
    m, k = x.shape
    n = w.shape[1]
    tm, tn = _tile(m, MM_ROWS), _tile(n, MM_COLS)
    ins = [x, w]
    specs = [pl.BlockSpec((tm, k), lambda i, j: (i, 0)), pl.BlockSpec((k, tn), lambda i, j: (0, j))]
    if w2 is not None:
        ins.append(w2)
        specs.append(pl.BlockSpec((k, tn), lambda i, j: (0, j)))
    if b is not None:
        ins.append(b.reshape(1, n))
        specs.append(pl.BlockSpec((1, tn), lambda i, j: (0, j)))
    return pl.pallas_call(
        functools.partial(_mm_kernel, in_silu=in_silu, act=act, has_bias=b is not None, glu=w2 is not None),
        out_shape=jax.ShapeDtypeStruct((m, n), out_dtype),
        grid=(m // tm, n // tn),
        in_specs=specs,
        out_specs=pl.BlockSpec((tm, tn), lambda i, j: (i, j)),
        compiler_params=_params("parallel", "arbitrary"),
        name=name,
    )(*ins)


def _norm_mod_kernel(x_ref, g_ref, sc_ref, sh_ref, o_ref):
    y = _rms(x_ref[0], g_ref[...])
    o_ref[0] = (y * (1.0 + sc_ref[0]) + sh_ref[0]).astype(o_ref.dtype)


def _norm_mod(x3, g, sc, sh, per_row):
    G, R, D = x3.shape
    tr = _tile(R, ROW_TILE)
    row = pl.BlockSpec((1, tr, D), lambda g_, r: (g_, r, 0))
    return pl.pallas_call(
        _norm_mod_kernel,
        out_shape=jax.ShapeDtypeStruct((G, R, D), BF16),
        grid=(G, R // tr),
        in_specs=[row, pl.BlockSpec((1, D), lambda g_, r: (0, 0)), _mod_spec(per_row, tr, D), _mod_spec(per_row, tr, D)],
        out_specs=row,
        compiler_params=_params("parallel", "parallel"),
        name="norm_mod",
    )(x3, g.reshape(1, D), sc, sh)


def _delta_kernel(q_ref, k_ref, v_ref, wq_ref, wk_ref, wv_ref, ba_ref, alog_ref, dt_ref, zs_ref, gn_ref,
                  dn_ref, s_ref, ext_ref, *, heads_per_step, n_heads, chunk):
    C, HB, H = chunk, heads_per_step, n_heads
    hb = pl.program_id(1)
    c = pl.program_id(2)

    @pl.when(c == 0)
    def _():
        s_ref[...] = jnp.zeros_like(s_ref)
        ext_ref[:, 0:SUBLANES, :] = jnp.zeros((3, SUBLANES, HB * HEAD_DIM), F32)

    def conv(i, x_ref, w_ref):
        ext_ref[i, SUBLANES:SUBLANES + C, :] = x_ref[0]
        w = w_ref[...]
        n_tap = w.shape[0]
        acc = w[n_tap - 1:n_tap] * ext_ref[i, SUBLANES:SUBLANES + C, :]
        for j in range(n_tap - 1):
            lo = SUBLANES - (n_tap - 1) + j
            acc = acc + w[j:j + 1] * ext_ref[i, lo:lo + C, :]
        ext_ref[i, SUBLANES - (n_tap - 1):SUBLANES, :] = ext_ref[i, SUBLANES + C - (n_tap - 1):SUBLANES + C, :]
        return _silu(acc)

    q = conv(0, q_ref, wq_ref)
    k = conv(1, k_ref, wk_ref)
    v = conv(2, v_ref, wv_ref)

    ba = ba_ref[0]
    beta_all = _sigmoid(ba)
    g_all = -jnp.exp(alog_ref[...]) * _softplus(ba + dt_ref[...])
    row = lax.broadcasted_iota(I32, (C, C), 0)
    col = lax.broadcasted_iota(I32, (C, C), 1)
    ge = row >= col
    gt = row > col
    gc_all = _fdot(ge.astype(F32), g_all)
    gc_t = gc_all.T
    lane = lax.broadcasted_iota(I32, (C, LANES), 1)
    sub = lax.broadcasted_iota(I32, (LANES, C), 0)
    gn = gn_ref[...]

    for hh in range(HB):
        head = hb * HB + hh
        sl = slice(hh * HEAD_DIM, (hh + 1) * HEAD_DIM)
        beta = jnp.sum(jnp.where(lane == head, beta_all, 0.0), axis=1, keepdims=True)
        gc = jnp.sum(jnp.where(lane == H + head, gc_all, 0.0), axis=1, keepdims=True)
        gc_row = jnp.sum(jnp.where(sub == H + head, gc_t, 0.0), axis=0, keepdims=True)
        qh, kh, vh = q[:, sl], k[:, sl], v[:, sl]
        qh = qh * lax.rsqrt(jnp.sum(qh * qh, axis=-1, keepdims=True) + NORM_EPS) * (HEAD_DIM ** -0.5)
        kh = kh * lax.rsqrt(jnp.sum(kh * kh, axis=-1, keepdims=True) + NORM_EPS)
        eg = jnp.exp(gc)
        kb = kh * beta
        decay = jnp.where(ge, jnp.exp(jnp.where(ge, gc - gc_row, 0.0)), 0.0)
        low = jnp.where(gt, _bdot_nt(kb, kh) * decay, 0.0)
        rhs = jnp.concatenate([vh * beta, kb * eg], axis=1)
        sol = rhs - _bdot(low, rhs)
        pw = low
        span = 2
        while span < C:
            pw = _bdot(pw, pw)
            sol = sol + _bdot(pw, sol)
            span *= 2
        u, w = sol[:, :HEAD_DIM], sol[:, HEAD_DIM:]
        s = s_ref[0, hh]
        r = _bdot(jnp.concatenate([w, qh * eg], axis=0), s)
        v_new = u - r[:C]
        o = r[C:] + _bdot(_bdot_nt(qh, kh) * decay, v_new)
        g_last = gc[C - 1:C, :]
        kd = kh * jnp.exp(g_last - gc)
        s_ref[0, hh] = s * jnp.exp(g_last) + _bdot(kd.T, v_new)
        on = _rms(o, gn) * zs_ref[0, :, sl].astype(F32)
        dn_ref[0, :, sl] = on.astype(dn_ref.dtype)


def _delta_prompt(qkv3, ba3, zs3, w_conv, alog_vec, dt_vec, g_norm, n_heads):
    B, T, W3 = qkv3.shape
    H = n_heads
    HB = H
    nHB = H // HB
    C = DELTA_CHUNK
    assert T % C == 0 and W3 == 3 * H * HEAD_DIM
    wblk = HB * HEAD_DIM
    qspec = lambda off: pl.BlockSpec((1, C, wblk), lambda b, h, c: (b, c, off + h))
    wspec = lambda off: pl.BlockSpec((w_conv.shape[0], wblk), lambda b, h, c: (0, off + h))
    vec = pl.BlockSpec((1, LANES), lambda b, h, c: (0, 0))
    return pl.pallas_call(
        functools.partial(_delta_kernel, heads_per_step=HB, n_heads=H, chunk=C),
        out_shape=(jax.ShapeDtypeStruct((B, T, H * HEAD_DIM), BF16),
                   jax.ShapeDtypeStruct((B, H, HEAD_DIM, HEAD_DIM), F32)),
        grid=(B, nHB, T // C),
        in_specs=[qspec(0), qspec(nHB), qspec(2 * nHB), wspec(0), wspec(nHB), wspec(2 * nHB),
                  pl.BlockSpec((1, C, LANES), lambda b, h, c: (b, c, 0)), vec, vec,
                  pl.BlockSpec((1, C, wblk), lambda b, h, c: (b, c, h)), vec],
        out_specs=(pl.BlockSpec((1, C, wblk), lambda b, h, c: (b, c, h)),
                   pl.BlockSpec((1, HB, HEAD_DIM, HEAD_DIM), lambda b, h, c: (b, h, 0, 0))),
        scratch_shapes=[pltpu.VMEM((3, SUBLANES + C, wblk), F32)],
        compiler_params=_params("parallel", "parallel", "arbitrary"),
        name="delta_prompt",
    )(qkv3, qkv3, qkv3, w_conv, w_conv, w_conv, ba3, alog_vec, dt_vec, zs3, g_norm.reshape(1, HEAD_DIM))


def _delta_step_kernel(x_ref, st_ref, w_ref, ba_ref, alog_ref, dt_ref, zs_ref, gn_ref, s_ref,
                       dn_ref, nst_ref, ns_ref, *, n_heads):
    H = n_heads
    SB = x_ref.shape[0]
    x = x_ref[...]
    w = w_ref[...]
    n_tap = w.shape[0]
    acc = w[n_tap - 1:n_tap] * x
    for j in range(n_tap - 1):
        acc = acc + w[j:j + 1] * st_ref[j]
    for j in range(n_tap - 2):
        nst_ref[j] = st_ref[j + 1]
    nst_ref[n_tap - 2] = x
    qkv = _silu(acc)
    ba = ba_ref[...]
    beta_all = _sigmoid(ba)
    a_all = jnp.exp(-jnp.exp(alog_ref[...]) * _softplus(ba + dt_ref[...]))
    gn = gn_ref[...]
    pad = jnp.zeros((HEAD_DIM - SB, HEAD_DIM), F32)
    KW = H * HEAD_DIM
    for hh in range(H):
        sl = slice(hh * HEAD_DIM, (hh + 1) * HEAD_DIM)
        qh, kh, vh = qkv[:, sl], qkv[:, KW + hh * HEAD_DIM:KW + (hh + 1) * HEAD_DIM], qkv[:, 2 * KW + hh * HEAD_DIM:2 * KW + (hh + 1) * HEAD_DIM]
        qh = qh * lax.rsqrt(jnp.sum(qh * qh, axis=-1, keepdims=True) + NORM_EPS) * (HEAD_DIM ** -0.5)
        kh = kh * lax.rsqrt(jnp.sum(kh * kh, axis=-1, keepdims=True) + NORM_EPS)
        q_t = jnp.concatenate([qh, pad], axis=0).T
        k_t = jnp.concatenate([kh, pad], axis=0).T
        rows = []
        for s in range(SB):
            st = s_ref[s, hh] * a_all[s:s + 1, H + hh:H + hh + 1]
            kcol = k_t[:, s:s + 1]
            delta = (vh[s:s + 1, :] - jnp.sum(kcol * st, axis=0, keepdims=True)) * beta_all[s:s + 1, hh:hh + 1]
            st = st + kcol * delta
            ns_ref[s, hh] = st
            rows.append(jnp.sum(q_t[:, s:s + 1] * st, axis=0, keepdims=True))
        o = jnp.concatenate(rows, axis=0)
        on = _rms(o, gn) * zs_ref[:, sl].astype(F32)
        dn_ref[:, sl] = on.astype(dn_ref.dtype)


def _delta_step(qkv, ba, zs, state_q_t, state_s, w_conv, alog_vec, dt_vec, g_norm, n_heads):
    B, W3 = qkv.shape
    H = n_heads
    SB = _tile(B, DECODE_SEQS)
    n_hist = state_q_t.shape[0]
    vec = pl.BlockSpec((1, LANES), lambda i: (0, 0))
    return pl.pallas_call(
        functools.partial(_delta_step_kernel, n_heads=H),
        out_shape=(jax.ShapeDtypeStruct((B, H * HEAD_DIM), BF16),
                   jax.ShapeDtypeStruct(state_q_t.shape, F32),
                   jax.ShapeDtypeStruct(state_s.shape, F32)),
        grid=(B // SB,),
        in_specs=[pl.BlockSpec((SB, W3), lambda i: (i, 0)),
                  pl.BlockSpec((n_hist, SB, W3), lambda i: (0, i, 0)),
                  pl.BlockSpec(w_conv.shape, lambda i: (0, 0)),
                  pl.BlockSpec((SB, LANES), lambda i: (i, 0)), vec, vec,
                  pl.BlockSpec((SB, H * HEAD_DIM), lambda i: (i, 0)), vec,
                  pl.BlockSpec((SB, H, HEAD_DIM, HEAD_DIM), lambda i: (i, 0, 0, 0))],
        out_specs=(pl.BlockSpec((SB, H * HEAD_DIM), lambda i: (i, 0)),
                   pl.BlockSpec((n_hist, SB, W3), lambda i: (0, i, 0)),
                   pl.BlockSpec((SB, H, HEAD_DIM, HEAD_DIM), lambda i: (i, 0, 0, 0))),
        compiler_params=_params("parallel"),
        name="delta_step",
    )(qkv, state_q_t, w_conv, ba, alog_vec, dt_vec, zs, g_norm.reshape(1, HEAD_DIM), state_s)


def _ln_silu(x, g, b):
    mu = jnp.mean(x, axis=-1, keepdims=True)
    xc = x - mu
    y = xc * lax.rsqrt(jnp.mean(xc * xc, axis=-1, keepdims=True) + NORM_EPS)
    return _silu(y * g + b)


def _glu_conv_kernel(u_ref, w_ref, b_ref, g_ref, bl_ref, o_ref, ext_ref, acc_ref, *, rows, halo):
    R = rows
    n_tap = w_ref.shape[0]
    D = u_ref.shape[2]
    base = halo - (n_tap - 1)

    @pl.when(pl.program_id(1) == 0)
    def _():
        ext_ref[0:halo, :] = jnp.zeros((halo, D), F32)

    ext_ref[halo:halo + R, :] = u_ref[0]

    def strip(ci, carry):
        cs = pl.ds(pl.multiple_of(ci * LANES, LANES), LANES)
        acc = jnp.zeros((R, LANES), F32) + b_ref[:, cs]
        for j in range(n_tap):
            acc = acc + w_ref[j:j + 1, cs] * ext_ref[base + j:base + j + R, cs]
        acc_ref[:, cs] = acc
        return carry

    lax.fori_loop(0, D // LANES, strip, 0)
    ext_ref[0:halo, :] = ext_ref[R:R + halo, :]
    o_ref[0] = _ln_silu(acc_ref[...], g_ref[...], bl_ref[...]).astype(o_ref.dtype)


def _glu_conv_prompt(u3, w_dw, b_dw, g_ln, b_ln):
    B, T, D = u3.shape
    R = _tile(T, CONV_ROWS)
    n_tap = w_dw.shape[0]
    halo = -(-(n_tap - 1) // SUBLANES) * SUBLANES
    assert R >= halo
    vec = pl.BlockSpec((1, D), lambda b, t: (0, 0))
    return pl.pallas_call(
        functools.partial(_glu_conv_kernel, rows=R, halo=halo),
        out_shape=jax.ShapeDtypeStruct((B, T, D), BF16),
        grid=(B, T // R),
        in_specs=[pl.BlockSpec((1, R, D), lambda b, t: (b, t, 0)),
                  pl.BlockSpec((n_tap, D), lambda b, t: (0, 0)), vec, vec, vec],
        out_specs=pl.BlockSpec((1, R, D), lambda b, t: (b, t, 0)),
        scratch_shapes=[pltpu.VMEM((halo + R, D), F32), pltpu.VMEM((R, D), F32)],
        compiler_params=_params("parallel", "arbitrary"),
        name="glu_conv_prompt",
    )(u3, w_dw, b_dw.reshape(1, D), g_ln.reshape(1, D), b_ln.reshape(1, D))


def _glu_conv_step_kernel(u_ref, st_ref, w_ref, b_ref, g_ref, bl_ref, o_ref, nst_ref):
    n_hist = st_ref.shape[0]
    u = u_ref[...]
    acc = b_ref[...] + w_ref[n_hist:n_hist + 1, :] * u
    for j in range(n_hist):
        acc = acc + w_ref[j:j + 1, :] * st_ref[j]
    for j in range(n_hist - 1):
        nst_ref[j] = st_ref[j + 1]
    nst_ref[n_hist - 1] = u
    o_ref[...] = _ln_silu(acc, g_ref[...], bl_ref[...]).astype(o_ref.dtype)


def _glu_conv_step(u, state_t, w_dw, b_dw, g_ln, b_ln):
    B, D = u.shape
    n_hist = state_t.shape[0]
    SB = _tile(B, 64)
    vec = pl.BlockSpec((1, D), lambda i: (0, 0))
    return pl.pallas_call(
        _glu_conv_step_kernel,
        out_shape=(jax.ShapeDtypeStruct((B, D), BF16), jax.ShapeDtypeStruct(state_t.shape, F32)),
        grid=(B // SB,),
        in_specs=[pl.BlockSpec((SB, D), lambda i: (i, 0)), pl.BlockSpec((n_hist, SB, D), lambda i: (0, i, 0)),
                  pl.BlockSpec(w_dw.shape, lambda i: (0, 0)), vec, vec, vec],
        out_specs=(pl.BlockSpec((SB, D), lambda i: (i, 0)), pl.BlockSpec((n_hist, SB, D), lambda i: (0, i, 0))),
        compiler_params=_params("parallel"),
        name="glu_conv_step",
    )(u, state_t, w_dw, b_dw.reshape(1, D), g_ln.reshape(1, D), b_ln.reshape(1, D))


def _merge_out_kernel(cv_ref, dn_ref, mg_ref, x_ref, gt_ref, wpw_ref, bpw_ref, wout_ref, g_ref, o_ref,
                      wpw_s, wout_s):
    @pl.when((pl.program_id(0) == 0) & (pl.program_id(1) == 0))
    def _():
        wpw_s[...] = wpw_ref[...].astype(BF16)
        wout_s[...] = wout_ref[...].astype(BF16)

    D = dn_ref.shape[2]
    cvo = jnp.dot(cv_ref[0], wpw_s[...], preferred_element_type=F32) + bpw_ref[...]
    mg = mg_ref[0]
    merged = mg[:, :D].astype(F32) * dn_ref[0].astype(F32) + mg[:, D:].astype(F32) * cvo
    m = jnp.dot(merged.astype(BF16), wout_s[...], preferred_element_type=F32)
    o_ref[0] = x_ref[0] + gt_ref[0] * _rms(m, g_ref[...])


def _merge_out(cv3, dn3, mg3, x3, gt, w_pw, b_pw, w_out, g_post, per_row):
    G, R, D = x3.shape
    tr = _tile(R, ROW_TILE)
    row = lambda w: pl.BlockSpec((1, tr, w), lambda g_, r: (g_, r, 0))
    full = lambda a: pl.BlockSpec(a.shape, lambda g_, r: (0,) * a.ndim)
    b_pw, g_post = b_pw.reshape(1, D), g_post.reshape(1, D)
    return pl.pallas_call(
        _merge_out_kernel,
        out_shape=jax.ShapeDtypeStruct((G, R, D), F32),
        grid=(G, R // tr),
        in_specs=[row(D), row(D), row(2 * D), row(D), _mod_spec(per_row, tr, D),
                  full(w_pw), full(b_pw), full(w_out), full(g_post)],
        out_specs=row(D),
        scratch_shapes=[pltpu.VMEM(w_pw.shape, BF16), pltpu.VMEM(w_out.shape, BF16)],
        compiler_params=_params("arbitrary", "arbitrary"),
        name="merge_out",
    )(cv3, dn3, mg3, x3, gt, w_pw, b_pw, w_out, g_post)


def _router_kernel(x_ref, g_ref, sc_ref, sh_ref, wr_ref, br_ref, h_ref, idx_ref, gate_ref, *, n_experts):
    h = _rms(x_ref[0], g_ref[...]) * (1.0 + sc_ref[0]) + sh_ref[0]
    h_ref[0] = h
    tr = h.shape[0]
    lane = lax.broadcasted_iota(I32, (tr, LANES), 1)
    logits = jnp.where(lane < n_experts, _fdot(h, wr_ref[...]) + br_ref[...], NEG_BIG)
    vals, idxs = [], []
    for _ in range(TOP_K):
        m = jnp.max(logits, axis=-1, keepdims=True)
        i = jnp.min(jnp.where(logits == m, lane, LANES), axis=-1, keepdims=True)
        vals.append(m)
        idxs.append(i)
        logits = jnp.where(lane == i, NEG_BIG, logits)
    es = [jnp.exp(v - vals[0]) for v in vals]
    tot = es[0]
    for e in es[1:]:
        tot = tot + e
    idx_out = jnp.zeros((tr, LANES), I32)
    gate_out = jnp.zeros((tr, LANES), F32)
    for k in range(TOP_K):
        idx_out = jnp.where(lane == k, idxs[k], idx_out)
        gate_out = jnp.where(lane == k, es[k] / tot, gate_out)
    idx_ref[0] = idx_out
    gate_ref[0] = gate_out


def _router(x3, g, sc, sh, w_router, b_router, per_row):
    G, R, D = x3.shape
    E = w_router.shape[1]
    assert E <= LANES
    tr = _tile(R, ROW_TILE)
    wr = jnp.pad(w_router, ((0, 0), (0, LANES - E)))
    br = jnp.pad(b_router, (0, LANES - E)).reshape(1, LANES)
    row = lambda w: pl.BlockSpec((1, tr, w), lambda g_, r: (g_, r, 0))
    full = lambda a: pl.BlockSpec(a.shape, lambda g_, r: (0,) * a.ndim)
    g = g.reshape(1, D)
    return pl.pallas_call(
        functools.partial(_router_kernel, n_experts=E),
        out_shape=(jax.ShapeDtypeStruct((G, R, D), F32),
                   jax.ShapeDtypeStruct((G, R, LANES), I32), jax.ShapeDtypeStruct((G, R, LANES), F32)),
        grid=(G, R // tr),
        in_specs=[row(D), full(g), _mod_spec(per_row, tr, D), _mod_spec(per_row, tr, D), full(wr), full(br)],
        out_specs=(row(D), row(LANES), row(LANES)),
        compiler_params=_params("parallel", "parallel"),
        name="router",
    )(x3, g, sc, sh, wr, br)


def _expert_kernel(be_ref, nu_ref, tok0_ref, tok1_ref, h_hbm, wgu_ref, bgu_ref, wd_ref, bd_ref, y_ref,
                   xbuf, wgu_s, wd_s, sem, *, rows):
    b = pl.program_id(0)
    nb = pl.num_programs(0)
    n_used = nu_ref[0]
    slot = lax.rem(b, 2)

    def row_copy(tok_ref, r, sl):
        return pltpu.make_async_copy(h_hbm.at[pl.ds(tok_ref[0, 0, r], 1)], xbuf.at[sl, pl.ds(r, 1)], sem.at[sl])

    def start_rows(tok_ref, sl):
        def body(r, carry):
            row_copy(tok_ref, r, sl).start()
            return carry
        lax.fori_loop(0, rows, body, 0)

    @pl.when(b == 0)
    def _():
        start_rows(tok0_ref, 0)

    @pl.when((b + 1 < nb) & (b + 1 < n_used))
    def _():
        start_rows(tok1_ref, 1 - slot)

    @pl.when(b >= n_used)
    def _():
        y_ref[...] = jnp.zeros_like(y_ref)

    @pl.when(b < n_used)
    def _():
        def wait_body(r, carry):
            row_copy(tok0_ref, 0, slot).wait()
            return carry
        lax.fori_loop(0, rows, wait_body, 0)

        fresh = (b == 0) | (be_ref[b] != be_ref[jnp.maximum(b - 1, 0)])

        @pl.when(fresh)
        def _():
            wgu_s[...] = wgu_ref[0].astype(BF16)
            wd_s[...] = wd_ref[0].astype(BF16)

        F = wd_s.shape[0]
        x = xbuf[slot].astype(BF16)
        gu = jnp.dot(x, wgu_s[...], preferred_element_type=F32) + bgu_ref[0]
        gl = jnp.minimum(gu[:, :F], SWIGLU_LIMIT)
        up = jnp.clip(gu[:, F:], -SWIGLU_LIMIT, SWIGLU_LIMIT)
        act = gl * _sigmoid(SWIGLU_ALPHA * gl) * (up + 1.0)
        y_ref[...] = jnp.dot(act.astype(BF16), wd_s[...], preferred_element_type=F32) + bd_ref[0]


def _experts(h2d, block_e, n_used, row_tok, w_gate_up, b_gate_up, w_down, b_down):
    N, D = h2d.shape
    E, _, F2 = w_gate_up.shape
    F = w_down.shape[1]
    n_blocks = block_e.shape[0]
    rows = MOE_ROWS
    tok3 = row_tok.reshape(n_blocks, 1, rows)
    grid_spec = pltpu.PrefetchScalarGridSpec(
        num_scalar_prefetch=2,
        grid=(n_blocks,),
        in_specs=[
            pl.BlockSpec((1, 1, rows), lambda b, be, nu: (b, 0, 0), memory_space=pltpu.SMEM),
            pl.BlockSpec((1, 1, rows), lambda b, be, nu: (jnp.minimum(b + 1, n_blocks - 1), 0, 0), memory_space=pltpu.SMEM),
            pl.BlockSpec(memory_space=pl.ANY),
            pl.BlockSpec((1, D, F2), lambda b, be, nu: (be[b], 0, 0)),
            pl.BlockSpec((1, 1, F2), lambda b, be, nu: (be[b], 0, 0)),
            pl.BlockSpec((1, F, D), lambda b, be, nu: (be[b], 0, 0)),
            pl.BlockSpec((1, 1, D), lambda b, be, nu: (be[b], 0, 0)),
        ],
        out_specs=pl.BlockSpec((rows, D), lambda b, be, nu: (b, 0)),
        scratch_shapes=[pltpu.VMEM((2, rows, D), F32), pltpu.VMEM((D, F2), BF16), pltpu.VMEM((F, D), BF16),
                        pltpu.SemaphoreType.DMA((2,))],
    )
    return pl.pallas_call(
        functools.partial(_expert_kernel, rows=rows),
        out_shape=jax.ShapeDtypeStruct((n_blocks * rows, D), F32),
        grid_spec=grid_spec,
        compiler_params=_params("arbitrary"),
        name="experts",
    )(block_e, n_used, tok3, tok3, h2d, w_gate_up, b_gate_up.reshape(E, 1, F2), w_down, b_down.reshape(E, 1, D))


def _combine_kernel(pos0_ref, pos1_ref, y_hbm, gate_ref, x_ref, gt_ref, g_ref, o_ref, ybuf, sem, *, tokens):
    TT = tokens
    i = pl.program_id(0) * pl.num_programs(1) + pl.program_id(1)
    n = pl.num_programs(0) * pl.num_programs(1)
    slot = lax.rem(i, 2)

    def row_copy(pos_ref, j, sl):
        return pltpu.make_async_copy(y_hbm.at[pl.ds(pos_ref[0, 0, j], 1)], ybuf.at[sl, pl.ds(j, 1)], sem.at[sl])

    def start_rows(pos_ref, sl):
        def body(j, carry):
            row_copy(pos_ref, j, sl).start()
            return carry
        lax.fori_loop(0, TOP_K * TT, body, 0)

    @pl.when(i == 0)
    def _():
        start_rows(pos0_ref, 0)

    @pl.when(i + 1 < n)
    def _():
        start_rows(pos1_ref, 1 - slot)

    def wait_body(j, carry):
        row_copy(pos0_ref, 0, slot).wait()
        return carry
    lax.fori_loop(0, TOP_K * TT, wait_body, 0)

    gates = gate_ref[0]
    f = gates[:, 0:1] * ybuf[slot, 0:TT, :]
    for k in range(1, TOP_K):
        f = f + gates[:, k:k + 1] * ybuf[slot, k * TT:(k + 1) * TT, :]
    o_ref[0] = x_ref[0] + gt_ref[0] * _rms(f, g_ref[...])


def _combine(y_sorted, pos_tiles, gates3, x3, gt, g_post, per_row):
    G, R, D = x3.shape
    TT = _tile(R, COMBINE_TOKENS)
    nr = R // TT
    n_tiles = G * nr
    row = lambda w: pl.BlockSpec((1, TT, w), lambda g_, r: (g_, r, 0))
    return pl.pallas_call(
        functools.partial(_combine_kernel, tokens=TT),
        out_shape=jax.ShapeDtypeStruct((G, R, D), F32),
        grid=(G, nr),
        in_specs=[
            pl.BlockSpec((1, 1, TOP_K * TT), lambda g_, r: (g_ * nr + r, 0, 0), memory_space=pltpu.SMEM),
            pl.BlockSpec((1, 1, TOP_K * TT), lambda g_, r: (jnp.minimum(g_ * nr + r + 1, n_tiles - 1), 0, 0),
                         memory_space=pltpu.SMEM),
            pl.BlockSpec(memory_space=pl.ANY),
            row(LANES), row(D), _mod_spec(per_row, TT, D),
            pl.BlockSpec((1, D), lambda g_, r: (0, 0)),
        ],
        out_specs=row(D),
        scratch_shapes=[pltpu.VMEM((2, TOP_K * TT, D), F32), pltpu.SemaphoreType.DMA((2,))],
        compiler_params=_params("arbitrary", "arbitrary"),
        name="combine",
    )(pos_tiles, pos_tiles, y_sorted, gates3, x3, gt, g_post.reshape(1, D))


def _route_plan(idx, n_experts):
    N = idx.shape[0]
    M = N * TOP_K
    rows = MOE_ROWS
    n_blocks = -(-M // rows) + n_experts
    flat_e = idx.reshape(M)
    onehot = (flat_e[:, None] == jnp.arange(n_experts, dtype=I32)[None, :]).astype(I32)
    csum = jnp.cumsum(onehot, axis=0)
    rank = jnp.sum((csum - onehot) * onehot, axis=1)
    counts = csum[-1]
    blocks_per_e = (counts + rows - 1) // rows
    cum_blocks = jnp.cumsum(blocks_per_e)
    pad_start = (cum_blocks - blocks_per_e) * rows
    dest = pad_start[flat_e] + rank
    row_tok = jnp.zeros((n_blocks * rows,), I32).at[dest].set(jnp.arange(M, dtype=I32) // TOP_K)
    n_used = cum_blocks[-1]
    blk = jnp.minimum(jnp.arange(n_blocks, dtype=I32), n_used - 1)
    block_e = jnp.minimum(jnp.searchsorted(cum_blocks, blk, side="right"), n_experts - 1).astype(I32)
    return dest.reshape(N, TOP_K), row_tok, block_e, n_used.reshape(1).astype(I32)


def _moe(x3, sc, sh, gt, p, l, per_row):
    G, R, D = x3.shape
    N = G * R
    E = p["w_router"].shape[2]
    h3, idx3, gates3 = _router(x3, p["g_pre_ffn"][l], sc, sh, p["w_router"][l], p["b_router"][l], per_row)
    idx = idx3.reshape(N, LANES)[:, :TOP_K]
    dest, row_tok, block_e, n_used = _route_plan(idx, E)
    y_sorted = _experts(h3.reshape(N, D), block_e, n_used, row_tok, p["w_gate_up"][l], p["b_gate_up"][l],
                        p["w_down"][l], p["b_down"][l])
    TT = _tile(R, COMBINE_TOKENS)
    pos_tiles = dest.reshape(N // TT, TT, TOP_K).transpose(0, 2, 1).reshape(N // TT, 1, TOP_K * TT)
    return _combine(y_sorted, pos_tiles, gates3, x3, gt, p["g_post_ffn"][l], per_row)


def _trunk(x, c, s_dn, buf_qkv, buf_glu, p):
    B, T, D = x.shape
    depth = p["w_in"].shape[0]
    H = p["a_log"].shape[1]
    per_row = T == 1
    G, R = (1, B) if per_row else (B, T)
    N = B * T
    KW = H * HEAD_DIM
    QKV = 3 * KW
    x3 = x.reshape(G, R, D)
    mod = (lambda a: a.reshape(1, B, D)) if per_row else (lambda a: a.reshape(B, 1, D))
    new_s, new_q, new_g = [], [], []
    for l in range(depth):
        ada = _mm(c, p["w_ada"][l], F32, b=p["b_ada"][l], in_silu=True, name="ada")
        sh1, sc1, gt1, sh2, sc2, gt2 = (mod(ada[:, i * D:(i + 1) * D]) for i in range(6))
        h = _norm_mod(x3, p["g_pre_mix"][l], sc1, sh1, per_row).reshape(N, D)
        w_in = p["w_in"][l]
        o_z, o_b, o_glu, o_mg = QKV, QKV + KW, QKV + KW + 2 * H, QKV + KW + 2 * H + 2 * D
        qkv = _mm(h, w_in[:, :QKV], F32, name="proj_qkv")
        zs = _mm(h, w_in[:, o_z:o_b], BF16, act="silu", name="proj_z")
        w_ba = jnp.pad(w_in[:, o_b:o_glu], ((0, 0), (0, LANES - 2 * H)))
        ba = _mm(h, w_ba, F32, name="proj_ba")
        u = _mm(h, w_in[:, o_glu:o_glu + D], F32, w2=w_in[:, o_glu + D:o_mg], name="proj_glu")
        mg = _mm(h, w_in[:, o_mg:], BF16, act="sigmoid", name="proj_gate")
        alog_vec = jnp.pad(p["a_log"][l], (H, LANES - 2 * H)).reshape(1, LANES)
        dt_vec = jnp.pad(p["dt_bias"][l], (H, LANES - 2 * H)).reshape(1, LANES)
        if per_row:
            st_q = jnp.swapaxes(buf_qkv[l], 0, 1)
            dn, nst_q, s_new = _delta_step(qkv, ba, zs, st_q, s_dn[l], p["w_qkv_conv"][l], alog_vec, dt_vec,
                                           p["g_dn_norm"][l], H)
            bq = jnp.swapaxes(nst_q, 0, 1)
            st_g = jnp.swapaxes(buf_glu[l], 0, 1)
            cv, nst_g = _glu_conv_step(u, st_g, p["w_dw_conv"][l], p["b_dw_conv"][l], p["g_ln_conv"][l],
                                       p["b_ln_conv"][l])
            bg = jnp.swapaxes(nst_g, 0, 1)
        else:
            dn, s_new = _delta_prompt(qkv.reshape(B, T, QKV), ba.reshape(B, T, LANES), zs.reshape(B, T, KW),
                                      p["w_qkv_conv"][l], alog_vec, dt_vec, p["g_dn_norm"][l], H)
            n_q = p["w_qkv_conv"].shape[1] - 1
            bq = qkv.reshape(B, T, QKV)[:, T - n_q:, :]
            cv = _glu_conv_prompt(u.reshape(B, T, D), p["w_dw_conv"][l], p["b_dw_conv"][l], p["g_ln_conv"][l],
                                  p["b_ln_conv"][l])
            n_g = p["w_dw_conv"].shape[1] - 1
            bg = u.reshape(B, T, D)[:, T - n_g:, :]
        x3 = _merge_out(cv.reshape(G, R, D), dn.reshape(G, R, KW), mg.reshape(G, R, 2 * D), x3, gt1,
                        p["w_pw_conv"][l], p["b_pw_conv"][l], p["w_out"][l], p["g_post_mix"][l], per_row)
        x3 = _moe(x3, sc2, sh2, gt2, p, l, per_row)
        new_s.append(s_new)
        new_q.append(bq)
        new_g.append(bg)
    return x3.reshape(B, T, D), jnp.stack(new_s), jnp.stack(new_q), jnp.stack(new_g)


def kernel(x_prompt, x_sample, c_prompt, c_sample, state_dn, state_qkv_conv, state_glu_conv, w_ada, b_ada, g_pre_mix, g_post_mix, g_pre_ffn, g_post_ffn, w_in, w_qkv_conv, a_log, dt_bias, g_dn_norm, w_dw_conv, b_dw_conv, g_ln_conv, b_ln_conv, w_pw_conv, b_pw_conv, w_out, w_router, b_router, w_gate_up, b_gate_up, w_down, b_down):
    p = dict(w_ada=w_ada, b_ada=b_ada, g_pre_mix=g_pre_mix, g_post_mix=g_post_mix, g_pre_ffn=g_pre_ffn,
             g_post_ffn=g_post_ffn, w_in=w_in, w_qkv_conv=w_qkv_conv, a_log=a_log, dt_bias=dt_bias,
             g_dn_norm=g_dn_norm, w_dw_conv=w_dw_conv, b_dw_conv=b_dw_conv, g_ln_conv=g_ln_conv,
             b_ln_conv=b_ln_conv, w_pw_conv=w_pw_conv, b_pw_conv=b_pw_conv, w_out=w_out,
             w_router=w_router, b_router=b_router, w_gate_up=w_gate_up, b_gate_up=b_gate_up,
             w_down=w_down, b_down=b_down)
    y_p, s_p, q_p, g_p = _trunk(x_prompt, c_prompt, None, None, None, p)
    y_s, s_s, q_s, g_s = _trunk(x_sample, c_sample, state_dn, state_qkv_conv, state_glu_conv, p)
    return (y_p, y_s,
            s_p.astype(state_dn.dtype), q_p.astype(state_qkv_conv.dtype), g_p.astype(state_glu_conv.dtype),
            s_s.astype(state_dn.dtype), q_s.astype(state_qkv_conv.dtype), g_s.astype(state_glu_conv.dtype))
```

```python
import functools

import jax
import jax.numpy as jnp
from jax import lax
from jax.experimental import pallas as pl
from jax.experimental.pallas import tpu as pltpu

F32, BF16, I32 = jnp.float32, jnp.bfloat16, jnp.int32

NORM_EPS = 1e-6
HEAD_DIM = 128
TOP_K = 4
SWIGLU_LIMIT = 7.0
SWIGLU_ALPHA = 1.702

LANES = 128
SUBLANES = 8
VMEM_LIMIT_BYTES = 56 * 1024 * 1024

DELTA_CHUNK = 64
CONV_ROWS = 128
MM_ROWS = 1024
MM_COLS = 512
ROW_TILE = 512
MOE_ROWS = 256
DISPATCH_TOKENS = 128
COMBINE_TOKENS = 128
DMA_UNROLL = 8
DECODE_SEQS = 8

NEG_BIG = -1e30


def _params(*sem):
    return pltpu.CompilerParams(dimension_semantics=sem, vmem_limit_bytes=VMEM_LIMIT_BYTES)


def _sigmoid(x):
    return 1.0 / (1.0 + jnp.exp(-x))


def _silu(x):
    return x * _sigmoid(x)


def _softplus(x):
    return jnp.maximum(x, 0.0) + jnp.log(1.0 + jnp.exp(-jnp.abs(x)))


def _rms(x, g):
    return x * lax.rsqrt(jnp.mean(x * x, axis=-1, keepdims=True) + NORM_EPS) * g


def _bdot(a, b):
    return jnp.dot(a.astype(BF16), b.astype(BF16), preferred_element_type=F32)


def _bdot_nt(a, b):
    return lax.dot_general(a.astype(BF16), b.astype(BF16), (((1,), (1,)), ((), ())),
                           preferred_element_type=F32)


def _fdot(a, b):
    return jnp.dot(a, b, preferred_element_type=F32, precision=lax.Precision.HIGHEST)


def _tile(n, t):
    t = min(n, t)
    assert n % t == 0, (n, t)
    return t


def _mod_spec(per_row, tr, d):
    return pl.BlockSpec((1, tr if per_row else 1, d), (lambda g, r: (g, r, 0)) if per_row else (lambda g, r: (g, 0, 0)))


def _mm_kernel(*refs, in_silu, act, has_bias, glu):
    x_ref, w_ref = refs[0], refs[1]
    pos = 2
    w2_ref = None
    if glu:
        w2_ref = refs[pos]
        pos += 1
    b_ref = None
    if has_bias:
        b_ref = refs[pos]
        pos += 1
    o_ref = refs[pos]
    x = x_ref[...]
    if in_silu:
        x = _silu(x.astype(F32))
    x = x.astype(BF16)
    y = _bdot(x, w_ref[...])
    if has_bias:
        y = y + b_ref[...]
    if glu:
        y = y * _sigmoid(_bdot(x, w2_ref[...]))
    if act == "silu":
        y = _silu(y)
    elif act == "sigmoid":
        y = _sigmoid(y)
    o_ref[...] = y.astype(o_ref.dtype)


def _mm(x, w, out_dtype, *, layer, n, col=0, col2=None, b=None, act=None, in_silu=False, name="mm"):
    m, k = x.shape
    tm, tn = _tile(m, MM_ROWS), _tile(n, MM_COLS)
    assert col % tn == 0 and (col2 is None or col2 % tn == 0)
    wspec = lambda c0: pl.BlockSpec((None, k, tn), lambda i, j: (layer, 0, j + c0 // tn))
    ins = [x, w]
    specs = [pl.BlockSpec((tm, k), lambda i, j: (i, 0)), wspec(col)]
    if col2 is not None:
        ins.append(w)
        specs.append(wspec(col2))
    if b is not None:
        ins.append(b.reshape(b.shape[0], 1, b.shape[1]))
        specs.append(pl.BlockSpec((None, 1, tn), lambda i, j: (layer, 0, j + col // tn)))
    return pl.pallas_call(
        functools.partial(_mm_kernel, in_silu=in_silu, act=act, has_bias=b is not None, glu=col2 is not None),
        out_shape=jax.ShapeDtypeStruct((m, n), out_dtype),
        grid=(m // tm, n // tn),
        in_specs=specs,
        out_specs=pl.BlockSpec((tm, tn), lambda i, j: (i, j)),
        compiler_params=_params("parallel", "arbitrary"),
        name=name,
    )(*ins)


def _norm_mod_kernel(x_ref, g_ref, sc_ref, sh_ref, o_ref):
    y = _rms(x_ref[0], g_ref[...])
    o_ref[0] = (y * (1.0 + sc_ref[0]) + sh_ref[0]).astype(o_ref.dtype)


def _norm_mod(x3, g, sc, sh, per_row):
    G, R, D = x3.shape
    tr = _tile(R, ROW_TILE)
    row = pl.BlockSpec((1, tr, D), lambda g_, r: (g_, r, 0))
    return pl.pallas_call(
        _norm_mod_kernel,
        out_shape=jax.ShapeDtypeStruct((G, R, D), BF16),
        grid=(G, R // tr),
        in_specs=[row, pl.BlockSpec((1, D), lambda g_, r: (0, 0)), _mod_spec(per_row, tr, D), _mod_spec(per_row, tr, D)],
        out_specs=row,
        compiler_params=_params("parallel", "parallel"),
        name="norm_mod",
    )(x3, g.reshape(1, D), sc, sh)


def _delta_kernel(q_ref, k_ref, v_ref, wq_ref, wk_ref, wv_ref, ba_ref, alog_ref, dt_ref, zs_ref, gn_ref,
                  dn_ref, s_ref, ext_ref, *, heads_per_step, n_heads, chunk):
    C, HB, H = chunk, heads_per_step, n_heads
    hb = pl.program_id(1)
    c = pl.program_id(2)

    @pl.when(c == 0)
    def _():
        s_ref[...] = jnp.zeros_like(s_ref)
        ext_ref[:, 0:SUBLANES, :] = jnp.zeros((3, SUBLANES, HB * HEAD_DIM), F32)

    def conv(i, x_ref, w_ref):
        ext_ref[i, SUBLANES:SUBLANES + C, :] = x_ref[0]
        w = w_ref[...]
        n_tap = w.shape[0]
        acc = w[n_tap - 1:n_tap] * ext_ref[i, SUBLANES:SUBLANES + C, :]
        for j in range(n_tap - 1):
            lo = SUBLANES - (n_tap - 1) + j
            acc = acc + w[j:j + 1] * ext_ref[i, lo:lo + C, :]
        ext_ref[i, SUBLANES - (n_tap - 1):SUBLANES, :] = ext_ref[i, SUBLANES + C - (n_tap - 1):SUBLANES + C, :]
        return _silu(acc)

    q = conv(0, q_ref, wq_ref)
    k = conv(1, k_ref, wk_ref)
    v = conv(2, v_ref, wv_ref)

    ba = ba_ref[0]
    beta_all = _sigmoid(ba)
    g_all = -jnp.exp(alog_ref[...]) * _softplus(ba + dt_ref[...])
    row = lax.broadcasted_iota(I32, (C, C), 0)
    col = lax.broadcasted_iota(I32, (C, C), 1)
    ge = row >= col
    gt = row > col
    gc_all = _fdot(ge.astype(F32), g_all)
    gc_t = gc_all.T
    lane = lax.broadcasted_iota(I32, (C, LANES), 1)
    sub = lax.broadcasted_iota(I32, (LANES, C), 0)
    gn = gn_ref[...]

    hs = range(HB)
    sls = [slice(hh * HEAD_DIM, (hh + 1) * HEAD_DIM) for hh in hs]
    beta = [jnp.sum(jnp.where(lane == hb * HB + hh, beta_all, 0.0), axis=1, keepdims=True) for hh in hs]
    gc = [jnp.sum(jnp.where(lane == H + hb * HB + hh, gc_all, 0.0), axis=1, keepdims=True) for hh in hs]
    gc_row = [jnp.sum(jnp.where(sub == H + hb * HB + hh, gc_t, 0.0), axis=0, keepdims=True) for hh in hs]
    qh = [q[:, sl] for sl in sls]
    kh = [k[:, sl] for sl in sls]
    qh = [x * lax.rsqrt(jnp.sum(x * x, axis=-1, keepdims=True) + NORM_EPS) * (HEAD_DIM ** -0.5) for x in qh]
    kh = [x * lax.rsqrt(jnp.sum(x * x, axis=-1, keepdims=True) + NORM_EPS) for x in kh]
    eg = [jnp.exp(x) for x in gc]
    kb = [kh[hh] * beta[hh] for hh in hs]
    decay = [jnp.where(ge, jnp.exp(jnp.where(ge, gc[hh] - gc_row[hh], 0.0)), 0.0) for hh in hs]
    kk = [_bdot_nt(kb[hh], kh[hh]) for hh in hs]
    qk = [_bdot_nt(qh[hh], kh[hh]) for hh in hs]
    low = [jnp.where(gt, kk[hh] * decay[hh], 0.0) for hh in hs]
    rhs = [jnp.concatenate([v[:, sls[hh]] * beta[hh], kb[hh] * eg[hh]], axis=1) for hh in hs]
    sol = [rhs[hh] - _bdot(low[hh], rhs[hh]) for hh in hs]
    pw = low
    span = 2
    while span < C:
        pw = [_bdot(x, x) for x in pw]
        sol = [sol[hh] + _bdot(pw[hh], sol[hh]) for hh in hs]
        span *= 2
    s = [s_ref[0, hh] for hh in hs]
    r = [_bdot(jnp.concatenate([sol[hh][:, HEAD_DIM:], qh[hh] * eg[hh]], axis=0), s[hh]) for hh in hs]
    v_new = [sol[hh][:, :HEAD_DIM] - r[hh][:C] for hh in hs]
    o = [r[hh][C:] + _bdot(qk[hh] * decay[hh], v_new[hh]) for hh in hs]
    g_last = [x[C - 1:C, :] for x in gc]
    kd = [kh[hh] * jnp.exp(g_last[hh] - gc[hh]) for hh in hs]
    s_new = [s[hh] * jnp.exp(g_last[hh]) + _bdot(kd[hh].T, v_new[hh]) for hh in hs]
    for hh in hs:
        s_ref[0, hh] = s_new[hh]
        on = _rms(o[hh], gn) * zs_ref[0, :, sls[hh]].astype(F32)
        dn_ref[0, :, sls[hh]] = on.astype(dn_ref.dtype)


def _delta_prompt(qkv3, ba3, zs3, w_conv, alog_vec, dt_vec, g_norm, n_heads):
    B, T, W3 = qkv3.shape
    H = n_heads
    HB = H
    nHB = H // HB
    C = DELTA_CHUNK
    assert T % C == 0 and W3 == 3 * H * HEAD_DIM
    wblk = HB * HEAD_DIM
    qspec = lambda off: pl.BlockSpec((1, C, wblk), lambda b, h, c: (b, c, off + h))
    wspec = lambda off: pl.BlockSpec((w_conv.shape[0], wblk), lambda b, h, c: (0, off + h))
    vec = pl.BlockSpec((1, LANES), lambda b, h, c: (0, 0))
    return pl.pallas_call(
        functools.partial(_delta_kernel, heads_per_step=HB, n_heads=H, chunk=C),
        out_shape=(jax.ShapeDtypeStruct((B, T, H * HEAD_DIM), BF16),
                   jax.ShapeDtypeStruct((B, H, HEAD_DIM, HEAD_DIM), F32)),
        grid=(B, nHB, T // C),
        in_specs=[qspec(0), qspec(nHB), qspec(2 * nHB), wspec(0), wspec(nHB), wspec(2 * nHB),
                  pl.BlockSpec((1, C, LANES), lambda b, h, c: (b, c, 0)), vec, vec,
                  pl.BlockSpec((1, C, wblk), lambda b, h, c: (b, c, h)), vec],
        out_specs=(pl.BlockSpec((1, C, wblk), lambda b, h, c: (b, c, h)),
                   pl.BlockSpec((1, HB, HEAD_DIM, HEAD_DIM), lambda b, h, c: (b, h, 0, 0))),
        scratch_shapes=[pltpu.VMEM((3, SUBLANES + C, wblk), F32)],
        compiler_params=_params("parallel", "parallel", "arbitrary"),
        name="delta_prompt",
    )(qkv3, qkv3, qkv3, w_conv, w_conv, w_conv, ba3, alog_vec, dt_vec, zs3, g_norm.reshape(1, HEAD_DIM))


def _delta_step_kernel(x_ref, st_ref, w_ref, ba_ref, alog_ref, dt_ref, zs_ref, gn_ref, s_ref,
                       dn_ref, nst_ref, ns_ref, *, n_heads):
    H = n_heads
    SB = x_ref.shape[0]
    x = x_ref[...]
    w = w_ref[...]
    n_tap = w.shape[0]
    acc = w[n_tap - 1:n_tap] * x
    for j in range(n_tap - 1):
        acc = acc + w[j:j + 1] * st_ref[j]
    for j in range(n_tap - 2):
        nst_ref[j] = st_ref[j + 1]
    nst_ref[n_tap - 2] = x
    qkv = _silu(acc)
    ba = ba_ref[...]
    beta_all = _sigmoid(ba)
    a_all = jnp.exp(-jnp.exp(alog_ref[...]) * _softplus(ba + dt_ref[...]))
    gn = gn_ref[...]
    pad = jnp.zeros((HEAD_DIM - SB, HEAD_DIM), F32)
    KW = H * HEAD_DIM
    for hh in range(H):
        sl = slice(hh * HEAD_DIM, (hh + 1) * HEAD_DIM)
        qh, kh, vh = qkv[:, sl], qkv[:, KW + hh * HEAD_DIM:KW + (hh + 1) * HEAD_DIM], qkv[:, 2 * KW + hh * HEAD_DIM:2 * KW + (hh + 1) * HEAD_DIM]
        qh = qh * lax.rsqrt(jnp.sum(qh * qh, axis=-1, keepdims=True) + NORM_EPS) * (HEAD_DIM ** -0.5)
        kh = kh * lax.rsqrt(jnp.sum(kh * kh, axis=-1, keepdims=True) + NORM_EPS)
        q_t = jnp.concatenate([qh, pad], axis=0).T
        k_t = jnp.concatenate([kh, pad], axis=0).T
        rows = []
        for s in range(SB):
            st = s_ref[s, hh] * a_all[s:s + 1, H + hh:H + hh + 1]
            kcol = k_t[:, s:s + 1]
            delta = (vh[s:s + 1, :] - jnp.sum(kcol * st, axis=0, keepdims=True)) * beta_all[s:s + 1, hh:hh + 1]
            st = st + kcol * delta
            ns_ref[s, hh] = st
            rows.append(jnp.sum(q_t[:, s:s + 1] * st, axis=0, keepdims=True))
        o = jnp.concatenate(rows, axis=0)
        on = _rms(o, gn) * zs_ref[:, sl].astype(F32)
        dn_ref[:, sl] = on.astype(dn_ref.dtype)


def _delta_step(qkv, ba, zs, state_q_t, state_s, w_conv, alog_vec, dt_vec, g_norm, n_heads):
    B, W3 = qkv.shape
    H = n_heads
    SB = _tile(B, DECODE_SEQS)
    n_hist = state_q_t.shape[0]
    vec = pl.BlockSpec((1, LANES), lambda i: (0, 0))
    return pl.pallas_call(
        functools.partial(_delta_step_kernel, n_heads=H),
        out_shape=(jax.ShapeDtypeStruct((B, H * HEAD_DIM), BF16),
                   jax.ShapeDtypeStruct(state_q_t.shape, F32),
                   jax.ShapeDtypeStruct(state_s.shape, F32)),
        grid=(B // SB,),
        in_specs=[pl.BlockSpec((SB, W3), lambda i: (i, 0)),
                  pl.BlockSpec((n_hist, SB, W3), lambda i: (0, i, 0)),
                  pl.BlockSpec(w_conv.shape, lambda i: (0, 0)),
                  pl.BlockSpec((SB, LANES), lambda i: (i, 0)), vec, vec,
                  pl.BlockSpec((SB, H * HEAD_DIM), lambda i: (i, 0)), vec,
                  pl.BlockSpec((SB, H, HEAD_DIM, HEAD_DIM), lambda i: (i, 0, 0, 0))],
        out_specs=(pl.BlockSpec((SB, H * HEAD_DIM), lambda i: (i, 0)),
                   pl.BlockSpec((n_hist, SB, W3), lambda i: (0, i, 0)),
                   pl.BlockSpec((SB, H, HEAD_DIM, HEAD_DIM), lambda i: (i, 0, 0, 0))),
        compiler_params=_params("parallel"),
        name="delta_step",
    )(qkv, state_q_t, w_conv, ba, alog_vec, dt_vec, zs, g_norm.reshape(1, HEAD_DIM), state_s)


def _ln_silu(x, g, b):
    mu = jnp.mean(x, axis=-1, keepdims=True)
    xc = x - mu
    y = xc * lax.rsqrt(jnp.mean(xc * xc, axis=-1, keepdims=True) + NORM_EPS)
    return _silu(y * g + b)


def _glu_conv_kernel(u_ref, w_ref, b_ref, g_ref, bl_ref, o_ref, ext_ref, acc_ref, *, rows, halo):
    R = rows
    n_tap = w_ref.shape[0]
    D = u_ref.shape[2]
    base = halo - (n_tap - 1)

    @pl.when(pl.program_id(1) == 0)
    def _():
        ext_ref[0:halo, :] = jnp.zeros((halo, D), F32)

    ext_ref[halo:halo + R, :] = u_ref[0]

    def strip(ci, carry):
        cs = pl.ds(pl.multiple_of(ci * LANES, LANES), LANES)
        acc = jnp.zeros((R, LANES), F32) + b_ref[:, cs]
        for j in range(n_tap):
            acc = acc + w_ref[j:j + 1, cs] * ext_ref[base + j:base + j + R, cs]
        acc_ref[:, cs] = acc
        return carry

    lax.fori_loop(0, D // LANES, strip, 0)
    ext_ref[0:halo, :] = ext_ref[R:R + halo, :]
    o_ref[0] = _ln_silu(acc_ref[...], g_ref[...], bl_ref[...]).astype(o_ref.dtype)


def _glu_conv_prompt(u3, w_dw, b_dw, g_ln, b_ln):
    B, T, D = u3.shape
    R = _tile(T, CONV_ROWS)
    n_tap = w_dw.shape[0]
    halo = -(-(n_tap - 1) // SUBLANES) * SUBLANES
    assert R >= halo
    vec = pl.BlockSpec((1, D), lambda b, t: (0, 0))
    return pl.pallas_call(
        functools.partial(_glu_conv_kernel, rows=R, halo=halo),
        out_shape=jax.ShapeDtypeStruct((B, T, D), BF16),
        grid=(B, T // R),
        in_specs=[pl.BlockSpec((1, R, D), lambda b, t: (b, t, 0)),
                  pl.BlockSpec((n_tap, D), lambda b, t: (0, 0)), vec, vec, vec],
        out_specs=pl.BlockSpec((1, R, D), lambda b, t: (b, t, 0)),
        scratch_shapes=[pltpu.VMEM((halo + R, D), F32), pltpu.VMEM((R, D), F32)],
        compiler_params=_params("parallel", "arbitrary"),
        name="glu_conv_prompt",
    )(u3, w_dw, b_dw.reshape(1, D), g_ln.reshape(1, D), b_ln.reshape(1, D))


def _glu_conv_step_kernel(u_ref, st_ref, w_ref, b_ref, g_ref, bl_ref, o_ref, nst_ref):
    n_hist = st_ref.shape[0]
    u = u_ref[...]
    acc = b_ref[...] + w_ref[n_hist:n_hist + 1, :] * u
    for j in range(n_hist):
        acc = acc + w_ref[j:j + 1, :] * st_ref[j]
    for j in range(n_hist - 1):
        nst_ref[j] = st_ref[j + 1]
    nst_ref[n_hist - 1] = u
    o_ref[...] = _ln_silu(acc, g_ref[...], bl_ref[...]).astype(o_ref.dtype)


def _glu_conv_step(u, state_t, w_dw, b_dw, g_ln, b_ln):
    B, D = u.shape
    n_hist = state_t.shape[0]
    SB = _tile(B, 64)
    vec = pl.BlockSpec((1, D), lambda i: (0, 0))
    return pl.pallas_call(
        _glu_conv_step_kernel,
        out_shape=(jax.ShapeDtypeStruct((B, D), BF16), jax.ShapeDtypeStruct(state_t.shape, F32)),
        grid=(B // SB,),
        in_specs=[pl.BlockSpec((SB, D), lambda i: (i, 0)), pl.BlockSpec((n_hist, SB, D), lambda i: (0, i, 0)),
                  pl.BlockSpec(w_dw.shape, lambda i: (0, 0)), vec, vec, vec],
        out_specs=(pl.BlockSpec((SB, D), lambda i: (i, 0)), pl.BlockSpec((n_hist, SB, D), lambda i: (0, i, 0))),
        compiler_params=_params("parallel"),
        name="glu_conv_step",
    )(u, state_t, w_dw, b_dw.reshape(1, D), g_ln.reshape(1, D), b_ln.reshape(1, D))


def _merge_out_kernel(cv_ref, dn_ref, mg_ref, x_ref, gt_ref, wpw_ref, bpw_ref, wout_ref, g_ref, o_ref,
                      wpw_s, wout_s):
    @pl.when((pl.program_id(0) == 0) & (pl.program_id(1) == 0))
    def _():
        wpw_s[...] = wpw_ref[...].astype(BF16)
        wout_s[...] = wout_ref[...].astype(BF16)

    D = dn_ref.shape[2]
    cvo = jnp.dot(cv_ref[0], wpw_s[...], preferred_element_type=F32) + bpw_ref[...]
    mg = mg_ref[0]
    merged = mg[:, :D].astype(F32) * dn_ref[0].astype(F32) + mg[:, D:].astype(F32) * cvo
    m = jnp.dot(merged.astype(BF16), wout_s[...], preferred_element_type=F32)
    o_ref[0] = x_ref[0] + gt_ref[0] * _rms(m, g_ref[...])


def _merge_out(cv3, dn3, mg3, x3, gt, w_pw, b_pw, w_out, g_post, per_row, layer):
    G, R, D = x3.shape
    tr = _tile(R, ROW_TILE)
    row = lambda w: pl.BlockSpec((1, tr, w), lambda g_, r: (g_, r, 0))
    mat = pl.BlockSpec((None, D, D), lambda g_, r: (layer, 0, 0))
    vec = pl.BlockSpec((None, 1, D), lambda g_, r: (layer, 0, 0))
    L = w_pw.shape[0]
    return pl.pallas_call(
        _merge_out_kernel,
        out_shape=jax.ShapeDtypeStruct((G, R, D), F32),
        grid=(G, R // tr),
        in_specs=[row(D), row(D), row(2 * D), row(D), _mod_spec(per_row, tr, D), mat, vec, mat, vec],
        out_specs=row(D),
        scratch_shapes=[pltpu.VMEM((D, D), BF16), pltpu.VMEM((D, D), BF16)],
        compiler_params=_params("arbitrary", "arbitrary"),
        name="merge_out",
    )(cv3, dn3, mg3, x3, gt, w_pw, b_pw.reshape(L, 1, D), w_out, g_post.reshape(L, 1, D))


def _router_kernel(x_ref, g_ref, sc_ref, sh_ref, wr_ref, br_ref, cin_ref,
                   h_ref, idx_ref, gate_ref, rank_ref, cnt_ref, carry, *, n_experts):
    @pl.when((pl.program_id(0) == 0) & (pl.program_id(1) == 0))
    def _():
        carry[...] = cin_ref[...].astype(F32)

    h = _rms(x_ref[0], g_ref[...]) * (1.0 + sc_ref[0]) + sh_ref[0]
    h_ref[0] = h
    tr = h.shape[0]
    lane = lax.broadcasted_iota(I32, (tr, LANES), 1)
    logits = jnp.where(lane < n_experts, _fdot(h, wr_ref[...]) + br_ref[...], NEG_BIG)
    vals, idxs = [], []
    for _ in range(TOP_K):
        m = jnp.max(logits, axis=-1, keepdims=True)
        i = jnp.min(jnp.where(logits == m, lane, LANES), axis=-1, keepdims=True)
        vals.append(m)
        idxs.append(i)
        logits = jnp.where(lane == i, NEG_BIG, logits)
    es = [jnp.exp(v - vals[0]) for v in vals]
    tot = es[0]
    for e in es[1:]:
        tot = tot + e
    hot = [lane == i for i in idxs]
    hot_all = hot[0]
    for m in hot[1:]:
        hot_all = hot_all | m
    hot_all = hot_all.astype(BF16)
    r_i = lax.broadcasted_iota(I32, (tr, tr), 0)
    c_i = lax.broadcasted_iota(I32, (tr, tr), 1)
    before = jnp.dot((r_i > c_i).astype(BF16), hot_all, preferred_element_type=F32) + carry[...]
    ranks = [jnp.sum(jnp.where(m, before, 0.0), axis=-1, keepdims=True).astype(I32) for m in hot]
    carry[...] = carry[...] + jnp.sum(hot_all.astype(F32), axis=0, keepdims=True)
    cnt_ref[...] = carry[...].astype(I32)
    idx_out = jnp.zeros((tr, LANES), I32)
    rank_out = jnp.zeros((tr, LANES), I32)
    gate_out = jnp.zeros((tr, LANES), F32)
    for k in range(TOP_K):
        idx_out = jnp.where(lane == k, idxs[k], idx_out)
        rank_out = jnp.where(lane == k, ranks[k], rank_out)
        gate_out = jnp.where(lane == k, es[k] / tot, gate_out)
    idx_ref[0] = idx_out
    rank_ref[0] = rank_out
    gate_ref[0] = gate_out


def _router(x3, g, sc, sh, w_router, b_router, counts_in, per_row):
    G, R, D = x3.shape
    E = w_router.shape[1]
    assert E <= LANES
    tr = _tile(R, ROW_TILE)
    wr = jnp.pad(w_router, ((0, 0), (0, LANES - E)))
    br = jnp.pad(b_router, (0, LANES - E)).reshape(1, LANES)
    row = lambda w: pl.BlockSpec((1, tr, w), lambda g_, r: (g_, r, 0))
    full = lambda a: pl.BlockSpec(a.shape, lambda g_, r: (0,) * a.ndim)
    g = g.reshape(1, D)
    return pl.pallas_call(
        functools.partial(_router_kernel, n_experts=E),
        out_shape=(jax.ShapeDtypeStruct((G, R, D), F32),
                   jax.ShapeDtypeStruct((G, R, LANES), I32), jax.ShapeDtypeStruct((G, R, LANES), F32),
                   jax.ShapeDtypeStruct((G, R, LANES), I32), jax.ShapeDtypeStruct((1, LANES), I32)),
        grid=(G, R // tr),
        in_specs=[row(D), full(g), _mod_spec(per_row, tr, D), _mod_spec(per_row, tr, D), full(wr), full(br),
                  full(counts_in)],
        out_specs=(row(D), row(LANES), row(LANES), row(LANES), pl.BlockSpec((1, LANES), lambda g_, r: (0, 0))),
        scratch_shapes=[pltpu.VMEM((1, LANES), F32)],
        compiler_params=_params("arbitrary", "arbitrary"),
        name="router",
    )(x3, g, sc, sh, wr, br, counts_in)


def _dispatch_kernel(*refs, tokens, tile_starts):
    n_groups = len(tile_starts) - 1
    pos_ref, h_refs = refs[0], refs[1:1 + n_groups]
    x_hbm, sem = refs[1 + n_groups], refs[2 + n_groups]
    i = pl.program_id(0)

    def row_copy(h_ref, k, t):
        return pltpu.make_async_copy(h_ref.at[pl.ds(t, 1)], x_hbm.at[pl.ds(pos_ref[0, 0, k * tokens + t], 1)],
                                     sem.at[0])

    for gi, h_ref in enumerate(h_refs):
        @pl.when((i >= tile_starts[gi]) & (i < tile_starts[gi + 1]))
        def _(h_ref=h_ref):
            def body(t, carry):
                for k in range(TOP_K):
                    row_copy(h_ref, k, t).start()
                return carry

            lax.fori_loop(0, tokens, body, 0, unroll=DMA_UNROLL // TOP_K)
            for _ in range(TOP_K * tokens):
                row_copy(h_ref, 0, 0).wait()


def _dispatch(hs, pos_tiles, n_rows):
    D = hs[0].shape[1]
    TT = DISPATCH_TOKENS
    tile_starts = [0]
    for h in hs:
        assert h.shape[0] % TT == 0
        tile_starts.append(tile_starts[-1] + h.shape[0] // TT)
    specs = [pl.BlockSpec((1, 1, TOP_K * TT), lambda i: (i, 0, 0), memory_space=pltpu.SMEM)]
    for gi, h in enumerate(hs):
        lo, n = tile_starts[gi], h.shape[0] // TT
        specs.append(pl.BlockSpec((TT, D), lambda i, lo=lo, n=n: (jnp.clip(i - lo, 0, n - 1), 0)))
    return pl.pallas_call(
        functools.partial(_dispatch_kernel, tokens=TT, tile_starts=tuple(tile_starts)),
        out_shape=jax.ShapeDtypeStruct((n_rows, D), F32),
        grid=(tile_starts[-1],),
        in_specs=specs,
        out_specs=pl.BlockSpec(memory_space=pl.ANY),
        scratch_shapes=[pltpu.SemaphoreType.DMA((1,))],
        compiler_params=_params("arbitrary"),
        name="dispatch",
    )(pos_tiles, *hs)


def _expert_kernel(ib_ref, ie_ref, lo_ref, hi_ref, x_ref, wgu_ref, bgu_ref, wd_ref, bd_ref, y_ref, wgu_s, wd_s):
    i = pl.program_id(0)
    prev = jnp.maximum(i - 1, 0)
    lo, hi = lo_ref[i], hi_ref[i]
    first = (i == 0) | (ib_ref[i] != ib_ref[prev])

    @pl.when(hi > lo)
    def _():
        @pl.when((i == 0) | (ie_ref[i] != ie_ref[prev]))
        def _():
            wgu_s[...] = wgu_ref[...].astype(BF16)
            wd_s[...] = wd_ref[...].astype(BF16)

        F = wd_s.shape[0]
        x = x_ref[...].astype(BF16)
        gu = jnp.dot(x, wgu_s[...], preferred_element_type=F32) + bgu_ref[...]
        gl = jnp.minimum(gu[:, :F], SWIGLU_LIMIT)
        up = jnp.clip(gu[:, F:], -SWIGLU_LIMIT, SWIGLU_LIMIT)
        act = gl * _sigmoid(SWIGLU_ALPHA * gl) * (up + 1.0)
        y = jnp.dot(act.astype(BF16), wd_s[...], preferred_element_type=F32) + bd_ref[...]
        rid = lax.broadcasted_iota(I32, (y.shape[0], 1), 0)
        mine = (rid >= lo) & (rid < hi)

        @pl.when(first)
        def _():
            y_ref[...] = jnp.where(mine, y, 0.0)

        @pl.when(jnp.logical_not(first))
        def _():
            y_ref[...] = jnp.where(mine, y, y_ref[...])


def _experts(x_sorted, items, w_gate_up, b_gate_up, w_down, b_down, layer):
    n_rows, D = x_sorted.shape
    L, E, _, F2 = w_gate_up.shape
    F = w_down.shape[2]
    rows = MOE_ROWS
    n_items = items[0].shape[0]
    grid_spec = pltpu.PrefetchScalarGridSpec(
        num_scalar_prefetch=4,
        grid=(n_items,),
        in_specs=[
            pl.BlockSpec((rows, D), lambda i, ib, ie, lo, hi: (ib[i], 0)),
            pl.BlockSpec((None, None, D, F2), lambda i, ib, ie, lo, hi: (layer, ie[i], 0, 0)),
            pl.BlockSpec((None, None, 1, F2), lambda i, ib, ie, lo, hi: (layer, ie[i], 0, 0)),
            pl.BlockSpec((None, None, F, D), lambda i, ib, ie, lo, hi: (layer, ie[i], 0, 0)),
            pl.BlockSpec((None, None, 1, D), lambda i, ib, ie, lo, hi: (layer, ie[i], 0, 0)),
        ],
        out_specs=pl.BlockSpec((rows, D), lambda i, ib, ie, lo, hi: (ib[i], 0)),
        scratch_shapes=[pltpu.VMEM((D, F2), BF16), pltpu.VMEM((F, D), BF16)],
    )
    return pl.pallas_call(
        _expert_kernel,
        out_shape=jax.ShapeDtypeStruct((n_rows, D), F32),
        grid_spec=grid_spec,
        compiler_params=_params("arbitrary"),
        name="experts",
    )(*items, x_sorted, w_gate_up, b_gate_up.reshape(L, E, 1, F2), w_down, b_down.reshape(L, E, 1, D))


def _combine_kernel(pos0_ref, pos1_ref, y_hbm, gate_ref, x_ref, gt_ref, g_ref, o_ref, ybuf, sem, *, tokens):
    TT = tokens
    n_rows = TOP_K * TT
    i = pl.program_id(0) * pl.num_programs(1) + pl.program_id(1)
    n = pl.num_programs(0) * pl.num_programs(1)
    slot = lax.rem(i, 2)

    def row_copy(pos_ref, k, t, sl):
        return pltpu.make_async_copy(y_hbm.at[pl.ds(pos_ref[0, 0, k * TT + t], 1)],
                                     ybuf.at[sl, pl.ds(k * TT + t, 1)], sem.at[sl])

    def start_rows(pos_ref, sl):
        def body(t, carry):
            for k in range(TOP_K):
                row_copy(pos_ref, k, t, sl).start()
            return carry
        lax.fori_loop(0, TT, body, 0, unroll=DMA_UNROLL // TOP_K)

    @pl.when(i == 0)
    def _():
        start_rows(pos0_ref, 0)

    @pl.when(i + 1 < n)
    def _():
        start_rows(pos1_ref, 1 - slot)

    for _ in range(n_rows):
        row_copy(pos0_ref, 0, 0, slot).wait()

    gates = gate_ref[0]
    f = gates[:, 0:1] * ybuf[slot, 0:TT, :]
    for k in range(1, TOP_K):
        f = f + gates[:, k:k + 1] * ybuf[slot, k * TT:(k + 1) * TT, :]
    o_ref[0] = x_ref[0] + gt_ref[0] * _rms(f, g_ref[...])


def _combine(y_sorted, pos_tiles, gates3, x3, gt, g_post, per_row):
    G, R, D = x3.shape
    TT = _tile(R, COMBINE_TOKENS)
    nr = R // TT
    n_tiles = G * nr
    row = lambda w: pl.BlockSpec((1, TT, w), lambda g_, r: (g_, r, 0))
    return pl.pallas_call(
        functools.partial(_combine_kernel, tokens=TT),
        out_shape=jax.ShapeDtypeStruct((G, R, D), F32),
        grid=(G, nr),
        in_specs=[
            pl.BlockSpec((1, 1, TOP_K * TT), lambda g_, r: (g_ * nr + r, 0, 0), memory_space=pltpu.SMEM),
            pl.BlockSpec((1, 1, TOP_K * TT), lambda g_, r: (jnp.minimum(g_ * nr + r + 1, n_tiles - 1), 0, 0),
                         memory_space=pltpu.SMEM),
            pl.BlockSpec(memory_space=pl.ANY),
            row(LANES), row(D), _mod_spec(per_row, TT, D),
            pl.BlockSpec((1, D), lambda g_, r: (0, 0)),
        ],
        out_specs=row(D),
        scratch_shapes=[pltpu.VMEM((2, TOP_K * TT, D), F32), pltpu.SemaphoreType.DMA((2,))],
        compiler_params=_params("arbitrary", "arbitrary"),
        name="combine",
    )(pos_tiles, pos_tiles, y_sorted, gates3, x3, gt, g_post.reshape(1, D))


def _pos_tiles(dest, tokens):
    n = dest.shape[0]
    return dest.reshape(n // tokens, tokens, TOP_K).transpose(0, 2, 1).reshape(n // tokens, 1, TOP_K * tokens)


def _work_items(counts, n_rows):
    E = counts.shape[0]
    rows = MOE_ROWS
    n_blocks = n_rows // rows
    n_items = n_blocks + E - 1
    end = jnp.cumsum(counts)
    start = end - counts
    first_blk = start // rows
    last_blk = jnp.maximum(end - 1, 0) // rows
    per_e = jnp.where(counts > 0, last_blk - first_blk + 1, 0)
    item_end = jnp.cumsum(per_e)
    item_start = item_end - per_e
    n_real = item_end[-1]
    i = jnp.minimum(jnp.arange(n_items, dtype=I32), n_real - 1)
    e = jnp.minimum(jnp.searchsorted(item_end, i, side="right"), E - 1).astype(I32)
    b = (first_blk[e] + i - item_start[e]).astype(I32)
    lo = jnp.maximum(start[e], b * rows) - b * rows
    hi = jnp.minimum(end[e], (b + 1) * rows) - b * rows
    hi = jnp.where(jnp.arange(n_items, dtype=I32) < n_real, hi, lo)
    return b, e, lo.astype(I32), hi.astype(I32)


def _moe(groups, p, l):
    E = p["w_router"].shape[2]
    D = groups[0]["x3"].shape[2]
    n_rows = TOP_K * sum(g["x3"].shape[0] * g["x3"].shape[1] for g in groups)
    assert n_rows % MOE_ROWS == 0
    counts = jnp.zeros((1, LANES), I32)
    routed = []
    for g in groups:
        h3, idx3, gates3, rank3, counts = _router(g["x3"], p["g_pre_ffn"][l], g["sc2"], g["sh2"],
                                                  p["w_router"][l], p["b_router"][l], counts, g["per_row"])
        routed.append((h3, idx3, gates3, rank3))
    counts = counts[0, :E]
    start = jnp.cumsum(counts) - counts
    dests = []
    for h3, idx3, gates3, rank3 in routed:
        G, R, _ = h3.shape
        idx = idx3.reshape(G * R, LANES)[:, :TOP_K]
        rank = rank3.reshape(G * R, LANES)[:, :TOP_K]
        hot = idx[:, :, None] == jnp.arange(E, dtype=I32)[None, None, :]
        dests.append(jnp.sum(jnp.where(hot, start[None, None, :], 0), axis=-1) + rank)
    x_sorted = _dispatch([r[0].reshape(-1, D) for r in routed],
                         jnp.concatenate([_pos_tiles(d, DISPATCH_TOKENS) for d in dests], axis=0), n_rows)
    y_sorted = _experts(x_sorted, _work_items(counts, n_rows), p["w_gate_up"], p["b_gate_up"], p["w_down"],
                        p["b_down"], l)
    outs = []
    for g, (h3, idx3, gates3, rank3), dest in zip(groups, routed, dests):
        R = h3.shape[1]
        outs.append(_combine(y_sorted, _pos_tiles(dest, _tile(R, COMBINE_TOKENS)), gates3, g["x3"], g["gt2"],
                             p["g_post_ffn"][l], g["per_row"]))
    return outs


def _mix(grp, p, l, w_tail, w_ba):
    x3, c, per_row = grp["x3"], grp["c"], grp["per_row"]
    B, T = grp["B"], grp["T"]
    G, R, D = x3.shape
    N = G * R
    H = p["a_log"].shape[1]
    KW = H * HEAD_DIM
    QKV = 3 * KW
    mod = (lambda a: a.reshape(1, B, D)) if per_row else (lambda a: a.reshape(B, 1, D))
    ada = _mm(c, p["w_ada"], F32, layer=l, n=6 * D, b=p["b_ada"], in_silu=True, name="ada")
    sh1, sc1, gt1, sh2, sc2, gt2 = (mod(ada[:, i * D:(i + 1) * D]) for i in range(6))
    h = _norm_mod(x3, p["g_pre_mix"][l], sc1, sh1, per_row).reshape(N, D)
    qkv = _mm(h, p["w_in"], F32, layer=l, n=QKV, name="proj_qkv")
    zs = _mm(h, p["w_in"], BF16, layer=l, n=KW, col=QKV, act="silu", name="proj_z")
    ba = _mm(h, w_ba, F32, layer=l, n=LANES, name="proj_ba")
    u = _mm(h, w_tail, F32, layer=l, n=D, col=0, col2=D, name="proj_glu")
    mg = _mm(h, w_tail, BF16, layer=l, n=2 * D, col=2 * D, act="sigmoid", name="proj_gate")
    alog_vec = jnp.pad(p["a_log"][l], (H, LANES - 2 * H)).reshape(1, LANES)
    dt_vec = jnp.pad(p["dt_bias"][l], (H, LANES - 2 * H)).reshape(1, LANES)
    conv_w = (p["w_dw_conv"][l], p["b_dw_conv"][l], p["g_ln_conv"][l], p["b_ln_conv"][l])
    if per_row:
        st_q = jnp.swapaxes(grp["buf_qkv"][l], 0, 1)
        dn, nst_q, s_new = _delta_step(qkv, ba, zs, st_q, grp["s_dn"][l], p["w_qkv_conv"][l], alog_vec, dt_vec,
                                       p["g_dn_norm"][l], H)
        bq = jnp.swapaxes(nst_q, 0, 1)
        cv, nst_g = _glu_conv_step(u, jnp.swapaxes(grp["buf_glu"][l], 0, 1), *conv_w)
        bg = jnp.swapaxes(nst_g, 0, 1)
    else:
        dn, s_new = _delta_prompt(qkv.reshape(B, T, QKV), ba.reshape(B, T, LANES), zs.reshape(B, T, KW),
                                  p["w_qkv_conv"][l], alog_vec, dt_vec, p["g_dn_norm"][l], H)
        bq = qkv.reshape(B, T, QKV)[:, T - (p["w_qkv_conv"].shape[1] - 1):, :]
        cv = _glu_conv_prompt(u.reshape(B, T, D), *conv_w)
        bg = u.reshape(B, T, D)[:, T - (p["w_dw_conv"].shape[1] - 1):, :]
    x3 = _merge_out(cv.reshape(G, R, D), dn.reshape(G, R, KW), mg.reshape(G, R, 2 * D), x3, gt1,
                    p["w_pw_conv"], p["b_pw_conv"], p["w_out"], p["g_post_mix"], per_row, l)
    return dict(grp, x3=x3, sc2=sc2, sh2=sh2, gt2=gt2), (s_new, bq, bg)


def kernel(x_prompt, x_sample, c_prompt, c_sample, state_dn, state_qkv_conv, state_glu_conv, w_ada, b_ada, g_pre_mix, g_post_mix, g_pre_ffn, g_post_ffn, w_in, w_qkv_conv, a_log, dt_bias, g_dn_norm, w_dw_conv, b_dw_conv, g_ln_conv, b_ln_conv, w_pw_conv, b_pw_conv, w_out, w_router, b_router, w_gate_up, b_gate_up, w_down, b_down):
    p = dict(w_ada=w_ada, b_ada=b_ada, g_pre_mix=g_pre_mix, g_post_mix=g_post_mix, g_pre_ffn=g_pre_ffn,
             g_post_ffn=g_post_ffn, w_in=w_in, w_qkv_conv=w_qkv_conv, a_log=a_log, dt_bias=dt_bias,
             g_dn_norm=g_dn_norm, w_dw_conv=w_dw_conv, b_dw_conv=b_dw_conv, g_ln_conv=g_ln_conv,
             b_ln_conv=b_ln_conv, w_pw_conv=w_pw_conv, b_pw_conv=b_pw_conv, w_out=w_out,
             w_router=w_router, b_router=b_router, w_gate_up=w_gate_up, b_gate_up=b_gate_up,
             w_down=w_down, b_down=b_down)
    depth, D = w_in.shape[0], w_in.shape[1]
    H = a_log.shape[1]
    o_ba = 4 * H * HEAD_DIM
    w_ba = jnp.pad(w_in[:, :, o_ba:o_ba + 2 * H], ((0, 0), (0, 0), (0, LANES - 2 * H)))
    w_tail = w_in[:, :, o_ba + 2 * H:]
    Bp, Tp, _ = x_prompt.shape
    Bs, Ts, _ = x_sample.shape
    assert Tp > 1 and Ts == 1
    groups = [dict(x3=x_prompt, c=c_prompt, per_row=False, B=Bp, T=Tp),
              dict(x3=x_sample.reshape(1, Bs, D), c=c_sample, per_row=True, B=Bs, T=Ts,
                   s_dn=state_dn, buf_qkv=state_qkv_conv, buf_glu=state_glu_conv)]
    states = [[], []]
    for l in range(depth):
        mixed = []
        for gi, grp in enumerate(groups):
            grp, st = _mix(grp, p, l, w_tail, w_ba)
            mixed.append(grp)
            states[gi].append(st)
        outs = _moe(mixed, p, l)
        groups = [dict(grp, x3=x3) for grp, x3 in zip(mixed, outs)]
    stack = lambda gi, j, like: jnp.stack([st[j] for st in states[gi]]).astype(like.dtype)
    return (groups[0]["x3"], groups[1]["x3"].reshape(Bs, Ts, D),
            stack(0, 0, state_dn), stack(0, 1, state_qkv_conv), stack(0, 2, state_glu_conv),
            stack(1, 0, state_dn), stack(1, 1, state_qkv_conv), stack(1, 2, state_glu_conv))
```

```python
import functools

import jax
import jax.numpy as jnp
from jax import lax
from jax.experimental import pallas as pl
from jax.experimental.pallas import tpu as pltpu

F32, BF16, I32 = jnp.float32, jnp.bfloat16, jnp.int32

NORM_EPS = 1e-6
HEAD_DIM = 128
TOP_K = 4
SWIGLU_LIMIT = 7.0
SWIGLU_ALPHA = 1.702

LANES = 128
SUBLANES = 8
VMEM_LIMIT_BYTES = 56 * 1024 * 1024

DELTA_CHUNK = 64
DELTA_SEQS = 2
CONV_ROWS = 128
MM_ROWS = 2048
MM_COLS = 512
ROW_TILE = 512
MOE_ROWS = 256
TOKEN_TILE = 128
DMA_UNROLL = 8
DECODE_SEQS = 8

NEG_BIG = -1e30


def _params(*sem):
    return pltpu.CompilerParams(dimension_semantics=sem, vmem_limit_bytes=VMEM_LIMIT_BYTES)


def _sigmoid(x):
    return 1.0 / (1.0 + jnp.exp(-x))


def _silu(x):
    return x * _sigmoid(x)


def _softplus(x):
    return jnp.maximum(x, 0.0) + jnp.log(1.0 + jnp.exp(-jnp.abs(x)))


def _rms(x, g):
    return x * lax.rsqrt(jnp.mean(x * x, axis=-1, keepdims=True) + NORM_EPS) * g


def _bdot(a, b):
    return jnp.dot(a.astype(BF16), b.astype(BF16), preferred_element_type=F32)


def _bdot_nt(a, b):
    return lax.dot_general(a.astype(BF16), b.astype(BF16), (((1,), (1,)), ((), ())),
                           preferred_element_type=F32)


def _fdot(a, b):
    return jnp.dot(a, b, preferred_element_type=F32, precision=lax.Precision.HIGHEST)


def _tile(n, t):
    t = min(n, t)
    assert n % t == 0, (n, t)
    return t


def _mod_spec(per_row, tr, d):
    return pl.BlockSpec((1, tr if per_row else 1, d), (lambda g, r: (g, r, 0)) if per_row else (lambda g, r: (g, 0, 0)))


def _mm_kernel(*refs, in_silu, act, has_bias, glu):
    x_ref, w_ref = refs[0], refs[1]
    pos = 2
    w2_ref = None
    if glu:
        w2_ref = refs[pos]
        pos += 1
    b_ref = None
    if has_bias:
        b_ref = refs[pos]
        pos += 1
    o_ref = refs[pos]
    x = x_ref[...]
    if in_silu:
        x = _silu(x.astype(F32))
    x = x.astype(BF16)
    y = _bdot(x, w_ref[...])
    if has_bias:
        y = y + b_ref[...]
    if glu:
        y = y * _sigmoid(_bdot(x, w2_ref[...]))
    if act == "silu":
        y = _silu(y)
    elif act == "sigmoid":
        y = _sigmoid(y)
    o_ref[...] = y.astype(o_ref.dtype)


def _mm(x, w, out_dtype, *, layer, n, col=0, col2=None, b=None, act=None, in_silu=False, name="mm"):
    m, k = x.shape
    tm, tn = _tile(m, MM_ROWS), _tile(n, MM_COLS)
    assert col % tn == 0 and (col2 is None or col2 % tn == 0)
    wspec = lambda c0: pl.BlockSpec((None, k, tn), lambda i, j: (layer, 0, j + c0 // tn))
    ins = [x, w]
    specs = [pl.BlockSpec((tm, k), lambda i, j: (i, 0)), wspec(col)]
    if col2 is not None:
        ins.append(w)
        specs.append(wspec(col2))
    if b is not None:
        ins.append(b.reshape(b.shape[0], 1, b.shape[1]))
        specs.append(pl.BlockSpec((None, 1, tn), lambda i, j: (layer, 0, j + col // tn)))
    return pl.pallas_call(
        functools.partial(_mm_kernel, in_silu=in_silu, act=act, has_bias=b is not None, glu=col2 is not None),
        out_shape=jax.ShapeDtypeStruct((m, n), out_dtype),
        grid=(m // tm, n // tn),
        in_specs=specs,
        out_specs=pl.BlockSpec((tm, tn), lambda i, j: (i, j)),
        compiler_params=_params("parallel", "arbitrary"),
        name=name,
    )(*ins)


def _norm_mod_kernel(x_ref, g_ref, sc_ref, sh_ref, o_ref):
    y = _rms(x_ref[0], g_ref[...])
    o_ref[0] = (y * (1.0 + sc_ref[0]) + sh_ref[0]).astype(o_ref.dtype)


def _norm_mod(x3, g, sc, sh, per_row):
    G, R, D = x3.shape
    tr = _tile(R, ROW_TILE)
    row = pl.BlockSpec((1, tr, D), lambda g_, r: (g_, r, 0))
    return pl.pallas_call(
        _norm_mod_kernel,
        out_shape=jax.ShapeDtypeStruct((G, R, D), BF16),
        grid=(G, R // tr),
        in_specs=[row, pl.BlockSpec((1, D), lambda g_, r: (0, 0)), _mod_spec(per_row, tr, D), _mod_spec(per_row, tr, D)],
        out_specs=row,
        compiler_params=_params("parallel", "parallel"),
        name="norm_mod",
    )(x3, g.reshape(1, D), sc, sh)


def _delta_kernel(q_ref, k_ref, v_ref, wq_ref, wk_ref, wv_ref, ba_ref, alog_ref, dt_ref, zs_ref, gn_ref,
                  dn_ref, s_ref, ext_ref, *, heads_per_step, n_heads, chunk):
    C, HB, H = chunk, heads_per_step, n_heads
    BB = q_ref.shape[0]
    hb = pl.program_id(1)
    c = pl.program_id(2)

    @pl.when(c == 0)
    def _():
        s_ref[...] = jnp.zeros_like(s_ref)
        ext_ref[:, 0:SUBLANES, :] = jnp.zeros((3 * BB, SUBLANES, HB * HEAD_DIM), F32)

    def conv(bb, i, x_ref, w_ref):
        e = 3 * bb + i
        ext_ref[e, SUBLANES:SUBLANES + C, :] = x_ref[bb]
        w = w_ref[...]
        n_tap = w.shape[0]
        acc = w[n_tap - 1:n_tap] * ext_ref[e, SUBLANES:SUBLANES + C, :]
        for j in range(n_tap - 1):
            lo = SUBLANES - (n_tap - 1) + j
            acc = acc + w[j:j + 1] * ext_ref[e, lo:lo + C, :]
        ext_ref[e, SUBLANES - (n_tap - 1):SUBLANES, :] = ext_ref[e, SUBLANES + C - (n_tap - 1):SUBLANES + C, :]
        return _silu(acc)

    q = [conv(bb, 0, q_ref, wq_ref) for bb in range(BB)]
    k = [conv(bb, 1, k_ref, wk_ref) for bb in range(BB)]
    v = [conv(bb, 2, v_ref, wv_ref) for bb in range(BB)]

    row = lax.broadcasted_iota(I32, (C, C), 0)
    col = lax.broadcasted_iota(I32, (C, C), 1)
    ge = row >= col
    gt = row > col
    ba = [ba_ref[bb] for bb in range(BB)]
    beta_all = [_sigmoid(x) for x in ba]
    gc_all = [_fdot(ge.astype(F32), -jnp.exp(alog_ref[...]) * _softplus(x + dt_ref[...])) for x in ba]
    gc_t = [x.T for x in gc_all]
    lane = lax.broadcasted_iota(I32, (C, LANES), 1)
    sub = lax.broadcasted_iota(I32, (LANES, C), 0)
    gn = gn_ref[...]

    units = [(bb, hh) for bb in range(BB) for hh in range(HB)]
    hs = range(len(units))
    sls = [slice(hh * HEAD_DIM, (hh + 1) * HEAD_DIM) for _, hh in units]
    beta = [jnp.sum(jnp.where(lane == hb * HB + hh, beta_all[bb], 0.0), axis=1, keepdims=True)
            for bb, hh in units]
    gc = [jnp.sum(jnp.where(lane == H + hb * HB + hh, gc_all[bb], 0.0), axis=1, keepdims=True)
          for bb, hh in units]
    gc_row = [jnp.sum(jnp.where(sub == H + hb * HB + hh, gc_t[bb], 0.0), axis=0, keepdims=True)
              for bb, hh in units]
    qh = [q[bb][:, sls[u]] for u, (bb, _) in enumerate(units)]
    kh = [k[bb][:, sls[u]] for u, (bb, _) in enumerate(units)]
    vh = [v[bb][:, sls[u]] for u, (bb, _) in enumerate(units)]
    qh = [x * lax.rsqrt(jnp.sum(x * x, axis=-1, keepdims=True) + NORM_EPS) * (HEAD_DIM ** -0.5) for x in qh]
    kh = [x * lax.rsqrt(jnp.sum(x * x, axis=-1, keepdims=True) + NORM_EPS) for x in kh]
    eg = [jnp.exp(x) for x in gc]
    kb = [kh[hh] * beta[hh] for hh in hs]
    decay = [jnp.where(ge, jnp.exp(jnp.where(ge, gc[hh] - gc_row[hh], 0.0)), 0.0) for hh in hs]
    kk = [_bdot_nt(kb[hh], kh[hh]) for hh in hs]
    qk = [_bdot_nt(qh[hh], kh[hh]) for hh in hs]
    low = [jnp.where(gt, kk[hh] * decay[hh], 0.0) for hh in hs]
    rhs = [jnp.concatenate([vh[hh] * beta[hh], kb[hh] * eg[hh]], axis=1) for hh in hs]
    sol = [rhs[hh] - _bdot(low[hh], rhs[hh]) for hh in hs]
    pw = low
    span = 2
    while span < C:
        pw = [_bdot(x, x) for x in pw]
        sol = [sol[hh] + _bdot(pw[hh], sol[hh]) for hh in hs]
        span *= 2
    s = [s_ref[bb, hh] for bb, hh in units]
    r = [_bdot(jnp.concatenate([sol[hh][:, HEAD_DIM:], qh[hh] * eg[hh]], axis=0), s[hh]) for hh in hs]
    v_new = [sol[hh][:, :HEAD_DIM] - r[hh][:C] for hh in hs]
    o = [r[hh][C:] + _bdot(qk[hh] * decay[hh], v_new[hh]) for hh in hs]
    g_last = [x[C - 1:C, :] for x in gc]
    kd = [kh[hh] * jnp.exp(g_last[hh] - gc[hh]) for hh in hs]
    s_new = [s[hh] * jnp.exp(g_last[hh]) + _bdot(kd[hh].T, v_new[hh]) for hh in hs]
    for u, (bb, hh) in enumerate(units):
        s_ref[bb, hh] = s_new[u]
        on = _rms(o[u], gn) * zs_ref[bb, :, sls[u]].astype(F32)
        dn_ref[bb, :, sls[u]] = on.astype(dn_ref.dtype)


def _delta_prompt(qkv3, ba3, zs3, w_conv, alog_vec, dt_vec, g_norm, n_heads):
    B, T, W3 = qkv3.shape
    H = n_heads
    HB = H
    nHB = H // HB
    C = DELTA_CHUNK
    BB = _tile(B, DELTA_SEQS)
    assert T % C == 0 and W3 == 3 * H * HEAD_DIM
    wblk = HB * HEAD_DIM
    qspec = lambda off: pl.BlockSpec((BB, C, wblk), lambda b, h, c: (b, c, off + h))
    wspec = lambda off: pl.BlockSpec((w_conv.shape[0], wblk), lambda b, h, c: (0, off + h))
    vec = pl.BlockSpec((1, LANES), lambda b, h, c: (0, 0))
    return pl.pallas_call(
        functools.partial(_delta_kernel, heads_per_step=HB, n_heads=H, chunk=C),
        out_shape=(jax.ShapeDtypeStruct((B, T, H * HEAD_DIM), F32),
                   jax.ShapeDtypeStruct((B, H, HEAD_DIM, HEAD_DIM), F32)),
        grid=(B // BB, nHB, T // C),
        in_specs=[qspec(0), qspec(nHB), qspec(2 * nHB), wspec(0), wspec(nHB), wspec(2 * nHB),
                  pl.BlockSpec((BB, C, LANES), lambda b, h, c: (b, c, 0)), vec, vec,
                  pl.BlockSpec((BB, C, wblk), lambda b, h, c: (b, c, h)), vec],
        out_specs=(pl.BlockSpec((BB, C, wblk), lambda b, h, c: (b, c, h)),
                   pl.BlockSpec((BB, HB, HEAD_DIM, HEAD_DIM), lambda b, h, c: (b, h, 0, 0))),
        scratch_shapes=[pltpu.VMEM((3 * BB, SUBLANES + C, wblk), F32)],
        compiler_params=_params("parallel", "parallel", "arbitrary"),
        name="delta_prompt",
    )(qkv3, qkv3, qkv3, w_conv, w_conv, w_conv, ba3, alog_vec, dt_vec, zs3, g_norm.reshape(1, HEAD_DIM))


def _delta_step_kernel(x_ref, st_ref, w_ref, ba_ref, alog_ref, dt_ref, zs_ref, gn_ref, s_ref,
                       dn_ref, nst_ref, ns_ref, *, n_heads):
    H = n_heads
    SB = x_ref.shape[0]
    x = x_ref[...]
    w = w_ref[...]
    n_tap = w.shape[0]
    acc = w[n_tap - 1:n_tap] * x
    for j in range(n_tap - 1):
        acc = acc + w[j:j + 1] * st_ref[j]
    for j in range(n_tap - 2):
        nst_ref[j] = st_ref[j + 1]
    nst_ref[n_tap - 2] = x
    qkv = _silu(acc)
    ba = ba_ref[...]
    beta_all = _sigmoid(ba)
    a_all = jnp.exp(-jnp.exp(alog_ref[...]) * _softplus(ba + dt_ref[...]))
    gn = gn_ref[...]
    pad = jnp.zeros((HEAD_DIM - SB, HEAD_DIM), F32)
    KW = H * HEAD_DIM
    for hh in range(H):
        sl = slice(hh * HEAD_DIM, (hh + 1) * HEAD_DIM)
        qh, kh, vh = qkv[:, sl], qkv[:, KW + hh * HEAD_DIM:KW + (hh + 1) * HEAD_DIM], qkv[:, 2 * KW + hh * HEAD_DIM:2 * KW + (hh + 1) * HEAD_DIM]
        qh = qh * lax.rsqrt(jnp.sum(qh * qh, axis=-1, keepdims=True) + NORM_EPS) * (HEAD_DIM ** -0.5)
        kh = kh * lax.rsqrt(jnp.sum(kh * kh, axis=-1, keepdims=True) + NORM_EPS)
        q_t = jnp.concatenate([qh, pad], axis=0).T
        k_t = jnp.concatenate([kh, pad], axis=0).T
        rnd = lambda a: a.astype(BF16).astype(F32)
        q_r, k_r = rnd(q_t), rnd(k_t)
        rows = []
        for s in range(SB):
            st = s_ref[s, hh] * a_all[s:s + 1, H + hh:H + hh + 1]
            ks = jnp.sum(k_r[:, s:s + 1] * rnd(st), axis=0, keepdims=True)
            delta = (vh[s:s + 1, :] - ks) * beta_all[s:s + 1, hh:hh + 1]
            st = st + k_t[:, s:s + 1] * delta
            ns_ref[s, hh] = st
            rows.append(jnp.sum(q_r[:, s:s + 1] * rnd(st), axis=0, keepdims=True))
        o = jnp.concatenate(rows, axis=0)
        on = _rms(o, gn) * zs_ref[:, sl].astype(F32)
        dn_ref[:, sl] = on.astype(dn_ref.dtype)


def _delta_step(qkv, ba, zs, state_q_t, state_s, w_conv, alog_vec, dt_vec, g_norm, n_heads):
    B, W3 = qkv.shape
    H = n_heads
    SB = _tile(B, DECODE_SEQS)
    n_hist = state_q_t.shape[0]
    vec = pl.BlockSpec((1, LANES), lambda i: (0, 0))
    return pl.pallas_call(
        functools.partial(_delta_step_kernel, n_heads=H),
        out_shape=(jax.ShapeDtypeStruct((B, H * HEAD_DIM), F32),
                   jax.ShapeDtypeStruct(state_q_t.shape, F32),
                   jax.ShapeDtypeStruct(state_s.shape, F32)),
        grid=(B // SB,),
        in_specs=[pl.BlockSpec((SB, W3), lambda i: (i, 0)),
                  pl.BlockSpec((n_hist, SB, W3), lambda i: (0, i, 0)),
                  pl.BlockSpec(w_conv.shape, lambda i: (0, 0)),
                  pl.BlockSpec((SB, LANES), lambda i: (i, 0)), vec, vec,
                  pl.BlockSpec((SB, H * HEAD_DIM), lambda i: (i, 0)), vec,
                  pl.BlockSpec((SB, H, HEAD_DIM, HEAD_DIM), lambda i: (i, 0, 0, 0))],
        out_specs=(pl.BlockSpec((SB, H * HEAD_DIM), lambda i: (i, 0)),
                   pl.BlockSpec((n_hist, SB, W3), lambda i: (0, i, 0)),
                   pl.BlockSpec((SB, H, HEAD_DIM, HEAD_DIM), lambda i: (i, 0, 0, 0))),
        compiler_params=_params("parallel"),
        name="delta_step",
    )(qkv, state_q_t, w_conv, ba, alog_vec, dt_vec, zs, g_norm.reshape(1, HEAD_DIM), state_s)


def _ln_silu(x, g, b):
    mu = jnp.mean(x, axis=-1, keepdims=True)
    xc = x - mu
    y = xc * lax.rsqrt(jnp.mean(xc * xc, axis=-1, keepdims=True) + NORM_EPS)
    return _silu(y * g + b)


def _glu_conv_kernel(u_ref, w_ref, b_ref, g_ref, bl_ref, o_ref, ext_ref, acc_ref, *, rows, halo):
    R = rows
    n_tap = w_ref.shape[0]
    D = u_ref.shape[2]
    base = halo - (n_tap - 1)

    @pl.when(pl.program_id(1) == 0)
    def _():
        ext_ref[0:halo, :] = jnp.zeros((halo, D), F32)

    ext_ref[halo:halo + R, :] = u_ref[0]

    def strip(ci, carry):
        cs = pl.ds(pl.multiple_of(ci * LANES, LANES), LANES)
        acc = jnp.zeros((R, LANES), F32) + b_ref[:, cs]
        for s in range(SUBLANES):
            ms = [m for m in range(base, base + n_tap) if m % SUBLANES == s]
            if not ms:
                continue
            n = R if s == 0 else R + SUBLANES
            part = None
            for m in ms:
                term = w_ref[m - base:m - base + 1, cs] * ext_ref[m - s:m - s + n, cs]
                part = term if part is None else part + term
            acc = acc + part[s:s + R]
        acc_ref[:, cs] = acc
        return carry

    lax.fori_loop(0, D // LANES, strip, 0)
    ext_ref[0:halo, :] = ext_ref[R:R + halo, :]
    o_ref[0] = _ln_silu(acc_ref[...], g_ref[...], bl_ref[...]).astype(o_ref.dtype)


def _glu_conv_prompt(u3, w_dw, b_dw, g_ln, b_ln):
    B, T, D = u3.shape
    R = _tile(T, CONV_ROWS)
    n_tap = w_dw.shape[0]
    halo = -(-(n_tap - 1) // SUBLANES) * SUBLANES
    assert R >= halo
    vec = pl.BlockSpec((1, D), lambda b, t: (0, 0))
    return pl.pallas_call(
        functools.partial(_glu_conv_kernel, rows=R, halo=halo),
        out_shape=jax.ShapeDtypeStruct((B, T, D), BF16),
        grid=(B, T // R),
        in_specs=[pl.BlockSpec((1, R, D), lambda b, t: (b, t, 0)),
                  pl.BlockSpec((n_tap, D), lambda b, t: (0, 0)), vec, vec, vec],
        out_specs=pl.BlockSpec((1, R, D), lambda b, t: (b, t, 0)),
        scratch_shapes=[pltpu.VMEM((halo + R, D), F32), pltpu.VMEM((R, D), F32)],
        compiler_params=_params("parallel", "arbitrary"),
        name="glu_conv_prompt",
    )(u3, w_dw, b_dw.reshape(1, D), g_ln.reshape(1, D), b_ln.reshape(1, D))


def _glu_conv_step_kernel(u_ref, st_ref, w_ref, b_ref, g_ref, bl_ref, o_ref, nst_ref):
    n_hist = st_ref.shape[0]
    u = u_ref[...]
    acc = b_ref[...] + w_ref[n_hist:n_hist + 1, :] * u
    for j in range(n_hist):
        acc = acc + w_ref[j:j + 1, :] * st_ref[j]
    for j in range(n_hist - 1):
        nst_ref[j] = st_ref[j + 1]
    nst_ref[n_hist - 1] = u
    o_ref[...] = _ln_silu(acc, g_ref[...], bl_ref[...]).astype(o_ref.dtype)


def _glu_conv_step(u, state_t, w_dw, b_dw, g_ln, b_ln):
    B, D = u.shape
    n_hist = state_t.shape[0]
    SB = _tile(B, 64)
    vec = pl.BlockSpec((1, D), lambda i: (0, 0))
    return pl.pallas_call(
        _glu_conv_step_kernel,
        out_shape=(jax.ShapeDtypeStruct((B, D), BF16), jax.ShapeDtypeStruct(state_t.shape, F32)),
        grid=(B // SB,),
        in_specs=[pl.BlockSpec((SB, D), lambda i: (i, 0)), pl.BlockSpec((n_hist, SB, D), lambda i: (0, i, 0)),
                  pl.BlockSpec(w_dw.shape, lambda i: (0, 0)), vec, vec, vec],
        out_specs=(pl.BlockSpec((SB, D), lambda i: (i, 0)), pl.BlockSpec((n_hist, SB, D), lambda i: (0, i, 0))),
        compiler_params=_params("parallel"),
        name="glu_conv_step",
    )(u, state_t, w_dw, b_dw.reshape(1, D), g_ln.reshape(1, D), b_ln.reshape(1, D))


def _merge_out_kernel(cv_ref, dn_ref, mg_ref, x_ref, gt_ref, wpw_ref, bpw_ref, wout_ref, g_ref, o_ref,
                      wpw_s, wout_s):
    @pl.when((pl.program_id(0) == 0) & (pl.program_id(1) == 0))
    def _():
        wpw_s[...] = wpw_ref[...].astype(BF16)
        wout_s[...] = wout_ref[...].astype(BF16)

    D = dn_ref.shape[2]
    cvo = jnp.dot(cv_ref[0], wpw_s[...], preferred_element_type=F32) + bpw_ref[...]
    mg = mg_ref[0]
    merged = mg[:, :D].astype(F32) * dn_ref[0].astype(F32) + mg[:, D:].astype(F32) * cvo
    m = jnp.dot(merged.astype(BF16), wout_s[...], preferred_element_type=F32)
    o_ref[0] = x_ref[0] + gt_ref[0] * _rms(m, g_ref[...])


def _merge_out(cv3, dn3, mg3, x3, gt, w_pw, b_pw, w_out, g_post, per_row, layer):
    G, R, D = x3.shape
    tr = _tile(R, ROW_TILE)
    row = lambda w: pl.BlockSpec((1, tr, w), lambda g_, r: (g_, r, 0))
    mat = pl.BlockSpec((None, D, D), lambda g_, r: (layer, 0, 0))
    vec = pl.BlockSpec((None, 1, D), lambda g_, r: (layer, 0, 0))
    L = w_pw.shape[0]
    return pl.pallas_call(
        _merge_out_kernel,
        out_shape=jax.ShapeDtypeStruct((G, R, D), F32),
        grid=(G, R // tr),
        in_specs=[row(D), row(D), row(2 * D), row(D), _mod_spec(per_row, tr, D), mat, vec, mat, vec],
        out_specs=row(D),
        scratch_shapes=[pltpu.VMEM((D, D), BF16), pltpu.VMEM((D, D), BF16)],
        compiler_params=_params("arbitrary", "arbitrary"),
        name="merge_out",
    )(cv3, dn3, mg3, x3, gt, w_pw, b_pw.reshape(L, 1, D), w_out, g_post.reshape(L, 1, D))


def _router_kernel(x_ref, g_ref, sc_ref, sh_ref, wr_ref, br_ref, cin_ref,
                   h_ref, idx_ref, gate_ref, rank_ref, cnt_ref, carry, *, n_experts):
    @pl.when((pl.program_id(0) == 0) & (pl.program_id(1) == 0))
    def _():
        carry[...] = cin_ref[...].astype(F32)

    h = _rms(x_ref[0], g_ref[...]) * (1.0 + sc_ref[0]) + sh_ref[0]
    h_ref[0] = h
    tr = h.shape[0]
    lane = lax.broadcasted_iota(I32, (tr, LANES), 1)
    logits = jnp.where(lane < n_experts, _bdot(h, wr_ref[...]) + br_ref[...], NEG_BIG)
    vals, idxs = [], []
    for _ in range(TOP_K):
        m = jnp.max(logits, axis=-1, keepdims=True)
        i = jnp.min(jnp.where(logits == m, lane, LANES), axis=-1, keepdims=True)
        vals.append(m)
        idxs.append(i)
        logits = jnp.where(lane == i, NEG_BIG, logits)
    es = [jnp.exp(v - vals[0]) for v in vals]
    tot = es[0]
    for e in es[1:]:
        tot = tot + e
    hot = [lane == i for i in idxs]
    hot_all = hot[0]
    for m in hot[1:]:
        hot_all = hot_all | m
    hot_all = hot_all.astype(BF16)
    r_i = lax.broadcasted_iota(I32, (tr, tr), 0)
    c_i = lax.broadcasted_iota(I32, (tr, tr), 1)
    before = jnp.dot((r_i > c_i).astype(BF16), hot_all, preferred_element_type=F32) + carry[...]
    ranks = [jnp.sum(jnp.where(m, before, 0.0), axis=-1, keepdims=True).astype(I32) for m in hot]
    carry[...] = carry[...] + jnp.sum(hot_all.astype(F32), axis=0, keepdims=True)
    cnt_ref[...] = carry[...].astype(I32)
    idx_out = jnp.zeros((tr, LANES), I32)
    rank_out = jnp.zeros((tr, LANES), I32)
    gate_out = jnp.zeros((tr, LANES), F32)
    for k in range(TOP_K):
        idx_out = jnp.where(lane == k, idxs[k], idx_out)
        rank_out = jnp.where(lane == k, ranks[k], rank_out)
        gate_out = jnp.where(lane == k, es[k] / tot, gate_out)
    idx_ref[0] = idx_out
    rank_ref[0] = rank_out
    gate_ref[0] = gate_out


def _router(x3, g, sc, sh, w_router, b_router, counts_in, per_row):
    G, R, D = x3.shape
    E = w_router.shape[1]
    assert E <= LANES
    tr = _tile(R, ROW_TILE)
    wr = jnp.pad(w_router, ((0, 0), (0, LANES - E)))
    br = jnp.pad(b_router, (0, LANES - E)).reshape(1, LANES)
    row = lambda w: pl.BlockSpec((1, tr, w), lambda g_, r: (g_, r, 0))
    full = lambda a: pl.BlockSpec(a.shape, lambda g_, r: (0,) * a.ndim)
    g = g.reshape(1, D)
    return pl.pallas_call(
        functools.partial(_router_kernel, n_experts=E),
        out_shape=(jax.ShapeDtypeStruct((G, R, D), F32),
                   jax.ShapeDtypeStruct((G, R, LANES), I32), jax.ShapeDtypeStruct((G, R, LANES), F32),
                   jax.ShapeDtypeStruct((G, R, LANES), I32), jax.ShapeDtypeStruct((1, LANES), I32)),
        grid=(G, R // tr),
        in_specs=[row(D), full(g), _mod_spec(per_row, tr, D), _mod_spec(per_row, tr, D), full(wr), full(br),
                  full(counts_in)],
        out_specs=(row(D), row(LANES), row(LANES), row(LANES), pl.BlockSpec((1, LANES), lambda g_, r: (0, 0))),
        scratch_shapes=[pltpu.VMEM((1, LANES), F32)],
        compiler_params=_params("arbitrary", "arbitrary"),
        name="router",
    )(x3, g, sc, sh, wr, br, counts_in)


def _dispatch_kernel(*refs, tokens, tile_starts):
    n_groups = len(tile_starts) - 1
    pos_ref, h_refs = refs[0], refs[1:1 + n_groups]
    x_hbm, sem = refs[1 + n_groups], refs[2 + n_groups]
    i = pl.program_id(0)

    def row_copy(h_ref, k, t):
        return pltpu.make_async_copy(h_ref.at[pl.ds(t, 1)], x_hbm.at[pl.ds(pos_ref[0, k, t], 1)], sem.at[0])

    for gi, h_ref in enumerate(h_refs):
        @pl.when((i >= tile_starts[gi]) & (i < tile_starts[gi + 1]))
        def _(h_ref=h_ref):
            def body(t, carry):
                for k in range(TOP_K):
                    row_copy(h_ref, k, t).start()
                return carry

            lax.fori_loop(0, tokens, body, 0, unroll=DMA_UNROLL // TOP_K)
            for _ in range(TOP_K * tokens):
                row_copy(h_ref, 0, 0).wait()


def _dispatch(hs, pos_tiles, n_rows):
    D = hs[0].shape[1]
    TT = TOKEN_TILE
    tile_starts = [0]
    for h in hs:
        assert h.shape[0] % TT == 0
        tile_starts.append(tile_starts[-1] + h.shape[0] // TT)
    specs = [pl.BlockSpec((1, SUBLANES, TT), lambda i: (i, 0, 0), memory_space=pltpu.SMEM)]
    for gi, h in enumerate(hs):
        lo, n = tile_starts[gi], h.shape[0] // TT
        specs.append(pl.BlockSpec((TT, D), lambda i, lo=lo, n=n: (jnp.clip(i - lo, 0, n - 1), 0)))
    return pl.pallas_call(
        functools.partial(_dispatch_kernel, tokens=TT, tile_starts=tuple(tile_starts)),
        out_shape=jax.ShapeDtypeStruct((n_rows, D), F32),
        grid=(tile_starts[-1],),
        in_specs=specs,
        out_specs=pl.BlockSpec(memory_space=pl.ANY),
        scratch_shapes=[pltpu.SemaphoreType.DMA((1,))],
        compiler_params=_params("arbitrary"),
        name="dispatch",
    )(pos_tiles, *hs)


def _expert_kernel(ib_ref, ie_ref, lo_ref, hi_ref, x_ref, wgu_ref, bgu_ref, wd_ref, bd_ref, y_ref, wgu_s, wd_s):
    i = pl.program_id(0)
    prev = jnp.maximum(i - 1, 0)
    lo, hi = lo_ref[i], hi_ref[i]
    first = (i == 0) | (ib_ref[i] != ib_ref[prev])

    @pl.when(hi > lo)
    def _():
        @pl.when((i == 0) | (ie_ref[i] != ie_ref[prev]))
        def _():
            wgu_s[...] = wgu_ref[...].astype(BF16)
            wd_s[...] = wd_ref[...].astype(BF16)

        F = wd_s.shape[0]
        x = x_ref[...].astype(BF16)
        gu = jnp.dot(x, wgu_s[...], preferred_element_type=F32) + bgu_ref[...]
        gl = jnp.minimum(gu[:, :F], SWIGLU_LIMIT)
        up = jnp.clip(gu[:, F:], -SWIGLU_LIMIT, SWIGLU_LIMIT)
        act = gl * _sigmoid(SWIGLU_ALPHA * gl) * (up + 1.0)
        y = jnp.dot(act.astype(BF16), wd_s[...], preferred_element_type=F32) + bd_ref[...]
        rid = lax.broadcasted_iota(I32, (y.shape[0], 1), 0)
        mine = (rid >= lo) & (rid < hi)

        @pl.when(first)
        def _():
            y_ref[...] = jnp.where(mine, y, 0.0)

        @pl.when(jnp.logical_not(first))
        def _():
            y_ref[...] = jnp.where(mine, y, y_ref[...])


def _experts(x_sorted, items, w_gate_up, b_gate_up, w_down, b_down, layer):
    n_rows, D = x_sorted.shape
    L, E, _, F2 = w_gate_up.shape
    F = w_down.shape[2]
    rows = MOE_ROWS
    n_items = items[0].shape[0]
    grid_spec = pltpu.PrefetchScalarGridSpec(
        num_scalar_prefetch=4,
        grid=(n_items,),
        in_specs=[
            pl.BlockSpec((rows, D), lambda i, ib, ie, lo, hi: (ib[i], 0)),
            pl.BlockSpec((None, None, D, F2), lambda i, ib, ie, lo, hi: (layer, ie[i], 0, 0)),
            pl.BlockSpec((None, None, 1, F2), lambda i, ib, ie, lo, hi: (layer, ie[i], 0, 0)),
            pl.BlockSpec((None, None, F, D), lambda i, ib, ie, lo, hi: (layer, ie[i], 0, 0)),
            pl.BlockSpec((None, None, 1, D), lambda i, ib, ie, lo, hi: (layer, ie[i], 0, 0)),
        ],
        out_specs=pl.BlockSpec((rows, D), lambda i, ib, ie, lo, hi: (ib[i], 0)),
        scratch_shapes=[pltpu.VMEM((D, F2), BF16), pltpu.VMEM((F, D), BF16)],
    )
    return pl.pallas_call(
        _expert_kernel,
        out_shape=jax.ShapeDtypeStruct((n_rows, D), F32),
        grid_spec=grid_spec,
        compiler_params=_params("arbitrary"),
        name="experts",
    )(*items, x_sorted, w_gate_up, b_gate_up.reshape(L, E, 1, F2), w_down, b_down.reshape(L, E, 1, D))


def _combine_kernel(pos0_ref, pos1_ref, y_hbm, gate_ref, x_ref, gt_ref, g_ref, o_ref, ybuf, sem, *, tokens):
    TT = tokens
    n_rows = TOP_K * TT
    i = pl.program_id(0) * pl.num_programs(1) + pl.program_id(1)
    n = pl.num_programs(0) * pl.num_programs(1)
    slot = lax.rem(i, 2)

    def row_copy(pos_ref, k, t, sl):
        return pltpu.make_async_copy(y_hbm.at[pl.ds(pos_ref[0, k, t], 1)],
                                     ybuf.at[sl, pl.ds(k * TT + t, 1)], sem.at[sl])

    def start_rows(pos_ref, sl):
        def body(t, carry):
            for k in range(TOP_K):
                row_copy(pos_ref, k, t, sl).start()
            return carry
        lax.fori_loop(0, TT, body, 0, unroll=DMA_UNROLL // TOP_K)

    @pl.when(i == 0)
    def _():
        start_rows(pos0_ref, 0)

    @pl.when(i + 1 < n)
    def _():
        start_rows(pos1_ref, 1 - slot)

    for _ in range(n_rows):
        row_copy(pos0_ref, 0, 0, slot).wait()

    gates = gate_ref[0]
    f = gates[:, 0:1] * ybuf[slot, 0:TT, :]
    for k in range(1, TOP_K):
        f = f + gates[:, k:k + 1] * ybuf[slot, k * TT:(k + 1) * TT, :]
    o_ref[0] = x_ref[0] + gt_ref[0] * _rms(f, g_ref[...])


def _combine(y_sorted, pos_tiles, gates3, x3, gt, g_post, per_row):
    G, R, D = x3.shape
    TT = TOKEN_TILE
    assert R % TT == 0
    nr = R // TT
    n_tiles = G * nr
    row = lambda w: pl.BlockSpec((1, TT, w), lambda g_, r: (g_, r, 0))
    cur = pl.BlockSpec((1, SUBLANES, TT), lambda g_, r: (g_ * nr + r, 0, 0), memory_space=pltpu.SMEM)
    nxt = pl.BlockSpec((1, SUBLANES, TT), lambda g_, r: (jnp.minimum(g_ * nr + r + 1, n_tiles - 1), 0, 0),
                       memory_space=pltpu.SMEM)
    return pl.pallas_call(
        functools.partial(_combine_kernel, tokens=TT),
        out_shape=jax.ShapeDtypeStruct((G, R, D), F32),
        grid=(G, nr),
        in_specs=[cur, nxt, pl.BlockSpec(memory_space=pl.ANY),
                  row(LANES), row(D), _mod_spec(per_row, TT, D), pl.BlockSpec((1, D), lambda g_, r: (0, 0))],
        out_specs=row(D),
        scratch_shapes=[pltpu.VMEM((2, TOP_K * TT, D), F32), pltpu.SemaphoreType.DMA((2,))],
        compiler_params=_params("arbitrary", "arbitrary"),
        name="combine",
    )(pos_tiles, pos_tiles, y_sorted, gates3, x3, gt, g_post.reshape(1, D))


def _dest_kernel(idx_ref, rank_ref, start_ref, o_ref):
    idx, rank = idx_ref[0], rank_ref[0].astype(F32)
    start = start_ref[...].astype(F32)
    lane = lax.broadcasted_iota(I32, idx.shape, 1)
    dest = jnp.zeros(idx.shape, F32)
    for k in range(TOP_K):
        first = jnp.sum(jnp.where(lane == idx[:, k:k + 1], start, 0.0), axis=1, keepdims=True)
        dest = jnp.where(lane == k, first + rank[:, k:k + 1], dest)
    o_ref[0] = dest.T[0:SUBLANES, :].astype(I32)


def _dest(idx3, rank3, start_vec):
    G, R, _ = idx3.shape
    TT = TOKEN_TILE
    assert TT == LANES and R % TT == 0 and TOP_K <= SUBLANES
    nr = R // TT
    blk = pl.BlockSpec((1, TT, LANES), lambda g_, r: (g_, r, 0))
    return pl.pallas_call(
        _dest_kernel,
        out_shape=jax.ShapeDtypeStruct((G * nr, SUBLANES, TT), I32),
        grid=(G, nr),
        in_specs=[blk, blk, pl.BlockSpec((1, LANES), lambda g_, r: (0, 0))],
        out_specs=pl.BlockSpec((1, SUBLANES, TT), lambda g_, r: (g_ * nr + r, 0, 0)),
        compiler_params=_params("parallel", "parallel"),
        name="dest",
    )(idx3, rank3, start_vec)


def _prefix_sums(v):
    n = v.shape[0]
    keep = jnp.arange(n, dtype=I32)[:, None] >= jnp.arange(n, dtype=I32)[None, :]
    return jnp.sum(jnp.where(keep, v[None, :], 0), axis=1)


def _work_items(counts, n_rows):
    E = counts.shape[0]
    rows = MOE_ROWS
    n_blocks = n_rows // rows
    n_items = n_blocks + E - 1
    end = _prefix_sums(counts)
    start = end - counts
    first_blk = start // rows
    last_blk = jnp.maximum(end - 1, 0) // rows
    per_e = jnp.where(counts > 0, last_blk - first_blk + 1, 0)
    item_end = _prefix_sums(per_e)
    item_start = item_end - per_e
    n_real = item_end[-1]
    i = jnp.clip(jnp.arange(n_items, dtype=I32), 0, jnp.maximum(n_real - 1, 0))
    e = jnp.minimum(jnp.sum((item_end[None, :] <= i[:, None]).astype(I32), axis=1), E - 1)
    b = (first_blk[e] + i - item_start[e]).astype(I32)
    lo = jnp.maximum(start[e], b * rows) - b * rows
    hi = jnp.minimum(end[e], (b + 1) * rows) - b * rows
    hi = jnp.where(jnp.arange(n_items, dtype=I32) < n_real, hi, lo)
    return b, e, lo.astype(I32), hi.astype(I32)


def _moe(groups, p, l):
    E = p["w_router"].shape[2]
    D = groups[0]["x3"].shape[2]
    n_rows = TOP_K * sum(g["x3"].shape[0] * g["x3"].shape[1] for g in groups)
    assert n_rows % MOE_ROWS == 0
    counts = jnp.zeros((1, LANES), I32)
    routed = []
    for g in groups:
        h3, idx3, gates3, rank3, counts = _router(g["x3"], p["g_pre_ffn"][l], g["sc2"], g["sh2"],
                                                  p["w_router"][l], p["b_router"][l], counts, g["per_row"])
        routed.append((h3, idx3, gates3, rank3))
    counts = counts[0, :E]
    start = _prefix_sums(counts) - counts
    start_vec = jnp.pad(start, (0, LANES - E)).reshape(1, LANES)
    tiles = [_dest(idx3, rank3, start_vec) for _, idx3, _, rank3 in routed]
    x_sorted = _dispatch([r[0].reshape(-1, D) for r in routed], jnp.concatenate(tiles, axis=0), n_rows)
    y_sorted = _experts(x_sorted, _work_items(counts, n_rows), p["w_gate_up"], p["b_gate_up"], p["w_down"],
                        p["b_down"], l)
    return [_combine(y_sorted, t, r[2], g["x3"], g["gt2"], p["g_post_ffn"][l], g["per_row"])
            for g, r, t in zip(groups, routed, tiles)]


def _mix(grp, p, l, w_tail, w_ba):
    x3, c, per_row = grp["x3"], grp["c"], grp["per_row"]
    B, T = grp["B"], grp["T"]
    G, R, D = x3.shape
    N = G * R
    H = p["a_log"].shape[1]
    KW = H * HEAD_DIM
    QKV = 3 * KW
    mod = (lambda a: a.reshape(1, B, D)) if per_row else (lambda a: a.reshape(B, 1, D))
    ada = _mm(c, p["w_ada"], F32, layer=l, n=6 * D, b=p["b_ada"], in_silu=True, name="ada")
    sh1, sc1, gt1, sh2, sc2, gt2 = (mod(ada[:, i * D:(i + 1) * D]) for i in range(6))
    h = _norm_mod(x3, p["g_pre_mix"][l], sc1, sh1, per_row).reshape(N, D)
    qkv = _mm(h, p["w_in"], F32, layer=l, n=QKV, name="proj_qkv")
    zs = _mm(h, p["w_in"], F32, layer=l, n=KW, col=QKV, act="silu", name="proj_z")
    ba = _mm(h, w_ba, F32, layer=l, n=LANES, name="proj_ba")
    u = _mm(h, w_tail, F32, layer=l, n=D, col=0, col2=D, name="proj_glu")
    mg = _mm(h, w_tail, F32, layer=l, n=2 * D, col=2 * D, act="sigmoid", name="proj_gate")
    alog_vec = jnp.pad(p["a_log"][l], (H, LANES - 2 * H)).reshape(1, LANES)
    dt_vec = jnp.pad(p["dt_bias"][l], (H, LANES - 2 * H)).reshape(1, LANES)
    conv_w = (p["w_dw_conv"][l], p["b_dw_conv"][l], p["g_ln_conv"][l], p["b_ln_conv"][l])
    if per_row:
        st_q = jnp.swapaxes(grp["buf_qkv"][l], 0, 1)
        dn, nst_q, s_new = _delta_step(qkv, ba, zs, st_q, grp["s_dn"][l], p["w_qkv_conv"][l], alog_vec, dt_vec,
                                       p["g_dn_norm"][l], H)
        bq = jnp.swapaxes(nst_q, 0, 1)
        cv, nst_g = _glu_conv_step(u, jnp.swapaxes(grp["buf_glu"][l], 0, 1), *conv_w)
        bg = jnp.swapaxes(nst_g, 0, 1)
    else:
        dn, s_new = _delta_prompt(qkv.reshape(B, T, QKV), ba.reshape(B, T, LANES), zs.reshape(B, T, KW),
                                  p["w_qkv_conv"][l], alog_vec, dt_vec, p["g_dn_norm"][l], H)
        bq = qkv.reshape(B, T, QKV)[:, T - (p["w_qkv_conv"].shape[1] - 1):, :]
        cv = _glu_conv_prompt(u.reshape(B, T, D), *conv_w)
        bg = u.reshape(B, T, D)[:, T - (p["w_dw_conv"].shape[1] - 1):, :]
    x3 = _merge_out(cv.reshape(G, R, D), dn.reshape(G, R, KW), mg.reshape(G, R, 2 * D), x3, gt1,
                    p["w_pw_conv"], p["b_pw_conv"], p["w_out"], p["g_post_mix"], per_row, l)
    return dict(grp, x3=x3, sc2=sc2, sh2=sh2, gt2=gt2), (s_new, bq, bg)


def kernel(x_prompt, x_sample, c_prompt, c_sample, state_dn, state_qkv_conv, state_glu_conv, w_ada, b_ada, g_pre_mix, g_post_mix, g_pre_ffn, g_post_ffn, w_in, w_qkv_conv, a_log, dt_bias, g_dn_norm, w_dw_conv, b_dw_conv, g_ln_conv, b_ln_conv, w_pw_conv, b_pw_conv, w_out, w_router, b_router, w_gate_up, b_gate_up, w_down, b_down):
    p = dict(w_ada=w_ada, b_ada=b_ada, g_pre_mix=g_pre_mix, g_post_mix=g_post_mix, g_pre_ffn=g_pre_ffn,
             g_post_ffn=g_post_ffn, w_in=w_in, w_qkv_conv=w_qkv_conv, a_log=a_log, dt_bias=dt_bias,
             g_dn_norm=g_dn_norm, w_dw_conv=w_dw_conv, b_dw_conv=b_dw_conv, g_ln_conv=g_ln_conv,
             b_ln_conv=b_ln_conv, w_pw_conv=w_pw_conv, b_pw_conv=b_pw_conv, w_out=w_out,
             w_router=w_router, b_router=b_router, w_gate_up=w_gate_up, b_gate_up=b_gate_up,
             w_down=w_down, b_down=b_down)
    depth, D = w_in.shape[0], w_in.shape[1]
    H = a_log.shape[1]
    o_ba = 4 * H * HEAD_DIM
    w_ba = jnp.pad(w_in[:, :, o_ba:o_ba + 2 * H], ((0, 0), (0, 0), (0, LANES - 2 * H)))
    w_tail = w_in[:, :, o_ba + 2 * H:]
    Bp, Tp, _ = x_prompt.shape
    Bs, Ts, _ = x_sample.shape
    assert Tp > 1 and Ts == 1
    groups = [dict(x3=x_prompt, c=c_prompt, per_row=False, B=Bp, T=Tp),
              dict(x3=x_sample.reshape(1, Bs, D), c=c_sample, per_row=True, B=Bs, T=Ts,
                   s_dn=state_dn, buf_qkv=state_qkv_conv, buf_glu=state_glu_conv)]
    states = [[], []]
    for l in range(depth):
        mixed = []
        for gi, grp in enumerate(groups):
            grp, st = _mix(grp, p, l, w_tail, w_ba)
            mixed.append(grp)
            states[gi].append(st)
        outs = _moe(mixed, p, l)
        groups = [dict(grp, x3=x3) for grp, x3 in zip(mixed, outs)]
    stack = lambda gi, j, like: jnp.stack([st[j] for st in states[gi]]).astype(like.dtype)
    return (groups[0]["x3"], groups[1]["x3"].reshape(Bs, Ts, D),
            stack(0, 0, state_dn), stack(0, 1, state_qkv_conv), stack(0, 2, state_glu_conv),
            stack(1, 0, state_dn), stack(1, 1, state_qkv_conv), stack(1, 2, state_glu_conv))
```

```python
import functools

import jax
import jax.numpy as jnp
from jax import lax
from jax.experimental import pallas as pl
from jax.experimental.pallas import tpu as pltpu

F32, BF16, I32 = jnp.float32, jnp.bfloat16, jnp.int32

NORM_EPS = 1e-6
HEAD_DIM = 128
TOP_K = 4
SWIGLU_LIMIT = 7.0
SWIGLU_ALPHA = 1.702

LANES = 128
SUBLANES = 8
VMEM_LIMIT_BYTES = 56 * 1024 * 1024

DELTA_CHUNK = 64
DELTA_SEQS = 2
CONV_ROWS = 128
MM_ROWS = 2048
MM_COLS = 512
ROW_TILE = 512
MOE_ROWS = 256
TOKEN_TILE = 128
DMA_UNROLL = 8
DECODE_SEQS = 8

NEG_BIG = -1e30


def _params(*sem):
    return pltpu.CompilerParams(dimension_semantics=sem, vmem_limit_bytes=VMEM_LIMIT_BYTES)


def _sigmoid(x):
    return 1.0 / (1.0 + jnp.exp(-x))


def _silu(x):
    return x * _sigmoid(x)


def _softplus(x):
    return jnp.maximum(x, 0.0) + jnp.log(1.0 + jnp.exp(-jnp.abs(x)))


def _rms(x, g):
    return x * lax.rsqrt(jnp.mean(x * x, axis=-1, keepdims=True) + NORM_EPS) * g


def _bdot(a, b):
    return jnp.dot(a.astype(BF16), b.astype(BF16), preferred_element_type=F32)


def _bdot_nt(a, b):
    return lax.dot_general(a.astype(BF16), b.astype(BF16), (((1,), (1,)), ((), ())),
                           preferred_element_type=F32)


def _fdot(a, b):
    return jnp.dot(a, b, preferred_element_type=F32, precision=lax.Precision.HIGHEST)


def _tile(n, t):
    t = min(n, t)
    assert n % t == 0, (n, t)
    return t


def _mod_spec(per_row, tr, d):
    return pl.BlockSpec((1, tr if per_row else 1, d), (lambda g, r: (g, r, 0)) if per_row else (lambda g, r: (g, 0, 0)))


def _load_row_tiles(ref, start, rows):
    return jnp.concatenate([ref[pl.ds(start * SUBLANES + c, rows, stride=SUBLANES), :] for c in range(SUBLANES)],
                           axis=1)


def _store_row_tiles(ref, start, x):
    for c in range(SUBLANES):
        ref[pl.ds(start * SUBLANES + c, x.shape[0], stride=SUBLANES), :] = x[:, c * LANES:(c + 1) * LANES]


def _row_tile(ref, r):
    return ref.at[pl.ds(pl.multiple_of(r * SUBLANES, SUBLANES), SUBLANES)]


def _mm_kernel(*refs, in_silu, act, has_bias, glu):
    x_ref, w_ref = refs[0], refs[1]
    pos = 2
    w2_ref = None
    if glu:
        w2_ref = refs[pos]
        pos += 1
    b_ref = None
    if has_bias:
        b_ref = refs[pos]
        pos += 1
    o_ref = refs[pos]
    x = x_ref[...]
    if in_silu:
        x = _silu(x.astype(F32))
    x = x.astype(BF16)
    y = _bdot(x, w_ref[...])
    if has_bias:
        y = y + b_ref[...]
    if glu:
        y = y * _sigmoid(_bdot(x, w2_ref[...]))
    if act == "silu":
        y = _silu(y)
    elif act == "sigmoid":
        y = _sigmoid(y)
    o_ref[...] = y.astype(o_ref.dtype)


def _mm(x, w, out_dtype, *, layer, n, col=0, col2=None, b=None, act=None, in_silu=False, name="mm"):
    m, k = x.shape
    tm, tn = _tile(m, MM_ROWS), _tile(n, MM_COLS)
    assert col % tn == 0 and (col2 is None or col2 % tn == 0)
    wspec = lambda c0: pl.BlockSpec((None, k, tn), lambda i, j: (layer, 0, j + c0 // tn))
    ins = [x, w]
    specs = [pl.BlockSpec((tm, k), lambda i, j: (i, 0)), wspec(col)]
    if col2 is not None:
        ins.append(w)
        specs.append(wspec(col2))
    if b is not None:
        ins.append(b.reshape(b.shape[0], 1, b.shape[1]))
        specs.append(pl.BlockSpec((None, 1, tn), lambda i, j: (layer, 0, j + col // tn)))
    return pl.pallas_call(
        functools.partial(_mm_kernel, in_silu=in_silu, act=act, has_bias=b is not None, glu=col2 is not None),
        out_shape=jax.ShapeDtypeStruct((m, n), out_dtype),
        grid=(m // tm, n // tn),
        in_specs=specs,
        out_specs=pl.BlockSpec((tm, tn), lambda i, j: (i, j)),
        compiler_params=_params("parallel", "arbitrary"),
        name=name,
    )(*ins)


def _norm_mod_kernel(x_ref, g_ref, sc_ref, sh_ref, o_ref):
    y = _rms(x_ref[0], g_ref[...])
    o_ref[0] = (y * (1.0 + sc_ref[0]) + sh_ref[0]).astype(o_ref.dtype)


def _norm_mod(x3, g, sc, sh, per_row):
    G, R, D = x3.shape
    tr = _tile(R, ROW_TILE)
    row = pl.BlockSpec((1, tr, D), lambda g_, r: (g_, r, 0))
    return pl.pallas_call(
        _norm_mod_kernel,
        out_shape=jax.ShapeDtypeStruct((G, R, D), BF16),
        grid=(G, R // tr),
        in_specs=[row, pl.BlockSpec((1, D), lambda g_, r: (0, 0)), _mod_spec(per_row, tr, D), _mod_spec(per_row, tr, D)],
        out_specs=row,
        compiler_params=_params("parallel", "parallel"),
        name="norm_mod",
    )(x3, g.reshape(1, D), sc, sh)


def _delta_kernel(q_ref, k_ref, v_ref, wq_ref, wk_ref, wv_ref, ba_ref, alog_ref, dt_ref, zs_ref, gn_ref,
                  dn_ref, s_ref, ext_ref, *, heads_per_step, n_heads, chunk):
    C, HB, H = chunk, heads_per_step, n_heads
    BB = q_ref.shape[0]
    hb = pl.program_id(1)
    c = pl.program_id(2)

    @pl.when(c == 0)
    def _():
        s_ref[...] = jnp.zeros_like(s_ref)
        ext_ref[:, 0:SUBLANES, :] = jnp.zeros((3 * BB, SUBLANES, HB * HEAD_DIM), F32)

    def conv(bb, i, x_ref, w_ref):
        e = 3 * bb + i
        ext_ref[e, SUBLANES:SUBLANES + C, :] = x_ref[bb]
        w = w_ref[...]
        n_tap = w.shape[0]
        acc = w[n_tap - 1:n_tap] * ext_ref[e, SUBLANES:SUBLANES + C, :]
        for j in range(n_tap - 1):
            lo = SUBLANES - (n_tap - 1) + j
            acc = acc + w[j:j + 1] * ext_ref[e, lo:lo + C, :]
        ext_ref[e, SUBLANES - (n_tap - 1):SUBLANES, :] = ext_ref[e, SUBLANES + C - (n_tap - 1):SUBLANES + C, :]
        return _silu(acc)

    q = [conv(bb, 0, q_ref, wq_ref) for bb in range(BB)]
    k = [conv(bb, 1, k_ref, wk_ref) for bb in range(BB)]
    v = [conv(bb, 2, v_ref, wv_ref) for bb in range(BB)]

    row = lax.broadcasted_iota(I32, (C, C), 0)
    col = lax.broadcasted_iota(I32, (C, C), 1)
    ge = row >= col
    gt = row > col
    ba = [ba_ref[bb] for bb in range(BB)]
    beta_all = [_sigmoid(x) for x in ba]
    gc_all = [_fdot(ge.astype(F32), -jnp.exp(alog_ref[...]) * _softplus(x + dt_ref[...])) for x in ba]
    gc_t = [x.T for x in gc_all]
    lane = lax.broadcasted_iota(I32, (C, LANES), 1)
    sub = lax.broadcasted_iota(I32, (LANES, C), 0)
    gn = gn_ref[...]

    units = [(bb, hh) for bb in range(BB) for hh in range(HB)]
    hs = range(len(units))
    sls = [slice(hh * HEAD_DIM, (hh + 1) * HEAD_DIM) for _, hh in units]
    beta = [jnp.sum(jnp.where(lane == hb * HB + hh, beta_all[bb], 0.0), axis=1, keepdims=True)
            for bb, hh in units]
    gc = [jnp.sum(jnp.where(lane == H + hb * HB + hh, gc_all[bb], 0.0), axis=1, keepdims=True)
          for bb, hh in units]
    gc_row = [jnp.sum(jnp.where(sub == H + hb * HB + hh, gc_t[bb], 0.0), axis=0, keepdims=True)
              for bb, hh in units]
    qh = [q[bb][:, sls[u]] for u, (bb, _) in enumerate(units)]
    kh = [k[bb][:, sls[u]] for u, (bb, _) in enumerate(units)]
    vh = [v[bb][:, sls[u]] for u, (bb, _) in enumerate(units)]
    qh = [x * lax.rsqrt(jnp.sum(x * x, axis=-1, keepdims=True) + NORM_EPS) * (HEAD_DIM ** -0.5) for x in qh]
    kh = [x * lax.rsqrt(jnp.sum(x * x, axis=-1, keepdims=True) + NORM_EPS) for x in kh]
    eg = [jnp.exp(x) for x in gc]
    kb = [kh[hh] * beta[hh] for hh in hs]
    decay = [jnp.where(ge, jnp.exp(jnp.where(ge, gc[hh] - gc_row[hh], 0.0)), 0.0) for hh in hs]
    kk = [_bdot_nt(kb[hh], kh[hh]) for hh in hs]
    qk = [_bdot_nt(qh[hh], kh[hh]) for hh in hs]
    low = [jnp.where(gt, kk[hh] * decay[hh], 0.0) for hh in hs]
    rhs = [jnp.concatenate([vh[hh] * beta[hh], kb[hh] * eg[hh]], axis=1) for hh in hs]
    sol = [rhs[hh] - _bdot(low[hh], rhs[hh]) for hh in hs]
    pw = low
    span = 2
    while span < C:
        pw = [_bdot(x, x) for x in pw]
        sol = [sol[hh] + _bdot(pw[hh], sol[hh]) for hh in hs]
        span *= 2
    s = [s_ref[bb, hh] for bb, hh in units]
    r = [_bdot(jnp.concatenate([sol[hh][:, HEAD_DIM:], qh[hh] * eg[hh]], axis=0), s[hh]) for hh in hs]
    v_new = [sol[hh][:, :HEAD_DIM] - r[hh][:C] for hh in hs]
    o = [r[hh][C:] + _bdot(qk[hh] * decay[hh], v_new[hh]) for hh in hs]
    g_last = [x[C - 1:C, :] for x in gc]
    kd = [kh[hh] * jnp.exp(g_last[hh] - gc[hh]) for hh in hs]
    s_new = [s[hh] * jnp.exp(g_last[hh]) + _bdot(kd[hh].T, v_new[hh]) for hh in hs]
    for u, (bb, hh) in enumerate(units):
        s_ref[bb, hh] = s_new[u]
        on = _rms(o[u], gn) * zs_ref[bb, :, sls[u]].astype(F32)
        dn_ref[bb, :, sls[u]] = on.astype(dn_ref.dtype)


def _delta_prompt(qkv3, ba3, zs3, w_conv, alog_vec, dt_vec, g_norm, n_heads):
    B, T, W3 = qkv3.shape
    H = n_heads
    HB = H
    nHB = H // HB
    C = DELTA_CHUNK
    BB = _tile(B, DELTA_SEQS)
    assert T % C == 0 and W3 == 3 * H * HEAD_DIM
    wblk = HB * HEAD_DIM
    qspec = lambda off: pl.BlockSpec((BB, C, wblk), lambda b, h, c: (b, c, off + h))
    wspec = lambda off: pl.BlockSpec((w_conv.shape[0], wblk), lambda b, h, c: (0, off + h))
    vec = pl.BlockSpec((1, LANES), lambda b, h, c: (0, 0))
    return pl.pallas_call(
        functools.partial(_delta_kernel, heads_per_step=HB, n_heads=H, chunk=C),
        out_shape=(jax.ShapeDtypeStruct((B, T, H * HEAD_DIM), F32),
                   jax.ShapeDtypeStruct((B, H, HEAD_DIM, HEAD_DIM), F32)),
        grid=(B // BB, nHB, T // C),
        in_specs=[qspec(0), qspec(nHB), qspec(2 * nHB), wspec(0), wspec(nHB), wspec(2 * nHB),
                  pl.BlockSpec((BB, C, LANES), lambda b, h, c: (b, c, 0)), vec, vec,
                  pl.BlockSpec((BB, C, wblk), lambda b, h, c: (b, c, h)), vec],
        out_specs=(pl.BlockSpec((BB, C, wblk), lambda b, h, c: (b, c, h)),
                   pl.BlockSpec((BB, HB, HEAD_DIM, HEAD_DIM), lambda b, h, c: (b, h, 0, 0))),
        scratch_shapes=[pltpu.VMEM((3 * BB, SUBLANES + C, wblk), F32)],
        compiler_params=_params("parallel", "parallel", "arbitrary"),
        name="delta_prompt",
    )(qkv3, qkv3, qkv3, w_conv, w_conv, w_conv, ba3, alog_vec, dt_vec, zs3, g_norm.reshape(1, HEAD_DIM))


def _delta_step_kernel(x_ref, st_ref, w_ref, ba_ref, alog_ref, dt_ref, zs_ref, gn_ref, s_ref,
                       dn_ref, nst_ref, ns_ref, *, n_heads):
    H = n_heads
    SB = x_ref.shape[0]
    x = x_ref[...]
    w = w_ref[...]
    n_tap = w.shape[0]
    acc = w[n_tap - 1:n_tap] * x
    for j in range(n_tap - 1):
        acc = acc + w[j:j + 1] * st_ref[j]
    for j in range(n_tap - 2):
        nst_ref[j] = st_ref[j + 1]
    nst_ref[n_tap - 2] = x
    qkv = _silu(acc)
    ba = ba_ref[...]
    beta_all = _sigmoid(ba)
    a_all = jnp.exp(-jnp.exp(alog_ref[...]) * _softplus(ba + dt_ref[...]))
    gn = gn_ref[...]
    pad = jnp.zeros((HEAD_DIM - SB, HEAD_DIM), F32)
    KW = H * HEAD_DIM
    for hh in range(H):
        sl = slice(hh * HEAD_DIM, (hh + 1) * HEAD_DIM)
        qh, kh, vh = qkv[:, sl], qkv[:, KW + hh * HEAD_DIM:KW + (hh + 1) * HEAD_DIM], qkv[:, 2 * KW + hh * HEAD_DIM:2 * KW + (hh + 1) * HEAD_DIM]
        qh = qh * lax.rsqrt(jnp.sum(qh * qh, axis=-1, keepdims=True) + NORM_EPS) * (HEAD_DIM ** -0.5)
        kh = kh * lax.rsqrt(jnp.sum(kh * kh, axis=-1, keepdims=True) + NORM_EPS)
        q_t = jnp.concatenate([qh, pad], axis=0).T
        k_t = jnp.concatenate([kh, pad], axis=0).T
        rnd = lambda a: a.astype(BF16).astype(F32)
        q_r, k_r = rnd(q_t), rnd(k_t)
        rows = []
        for s in range(SB):
            st = s_ref[s, hh] * a_all[s:s + 1, H + hh:H + hh + 1]
            ks = jnp.sum(k_r[:, s:s + 1] * rnd(st), axis=0, keepdims=True)
            delta = (vh[s:s + 1, :] - ks) * beta_all[s:s + 1, hh:hh + 1]
            st = st + k_t[:, s:s + 1] * delta
            ns_ref[s, hh] = st
            rows.append(jnp.sum(q_r[:, s:s + 1] * rnd(st), axis=0, keepdims=True))
        o = jnp.concatenate(rows, axis=0)
        on = _rms(o, gn) * zs_ref[:, sl].astype(F32)
        dn_ref[:, sl] = on.astype(dn_ref.dtype)


def _delta_step(qkv, ba, zs, state_q_t, state_s, w_conv, alog_vec, dt_vec, g_norm, n_heads):
    B, W3 = qkv.shape
    H = n_heads
    SB = _tile(B, DECODE_SEQS)
    n_hist = state_q_t.shape[0]
    vec = pl.BlockSpec((1, LANES), lambda i: (0, 0))
    return pl.pallas_call(
        functools.partial(_delta_step_kernel, n_heads=H),
        out_shape=(jax.ShapeDtypeStruct((B, H * HEAD_DIM), F32),
                   jax.ShapeDtypeStruct(state_q_t.shape, F32),
                   jax.ShapeDtypeStruct(state_s.shape, F32)),
        grid=(B // SB,),
        in_specs=[pl.BlockSpec((SB, W3), lambda i: (i, 0)),
                  pl.BlockSpec((n_hist, SB, W3), lambda i: (0, i, 0)),
                  pl.BlockSpec(w_conv.shape, lambda i: (0, 0)),
                  pl.BlockSpec((SB, LANES), lambda i: (i, 0)), vec, vec,
                  pl.BlockSpec((SB, H * HEAD_DIM), lambda i: (i, 0)), vec,
                  pl.BlockSpec((SB, H, HEAD_DIM, HEAD_DIM), lambda i: (i, 0, 0, 0))],
        out_specs=(pl.BlockSpec((SB, H * HEAD_DIM), lambda i: (i, 0)),
                   pl.BlockSpec((n_hist, SB, W3), lambda i: (0, i, 0)),
                   pl.BlockSpec((SB, H, HEAD_DIM, HEAD_DIM), lambda i: (i, 0, 0, 0))),
        compiler_params=_params("parallel"),
        name="delta_step",
    )(qkv, state_q_t, w_conv, ba, alog_vec, dt_vec, zs, g_norm.reshape(1, HEAD_DIM), state_s)


def _ln_silu(x, g, b):
    mu = jnp.mean(x, axis=-1, keepdims=True)
    xc = x - mu
    y = xc * lax.rsqrt(jnp.mean(xc * xc, axis=-1, keepdims=True) + NORM_EPS)
    return _silu(y * g + b)


def _glu_conv_kernel(u_ref, w_ref, b_ref, g_ref, bl_ref, o_ref, ext_ref, acc_ref, *, rows, halo):
    R = rows
    n_tap = w_ref.shape[0]
    D = u_ref.shape[2]
    base = halo - (n_tap - 1)

    @pl.when(pl.program_id(1) == 0)
    def _():
        ext_ref[0:halo, :] = jnp.zeros((halo, D), F32)

    ext_ref[halo:halo + R, :] = u_ref[0]

    def strip(ci, carry):
        cs = pl.ds(pl.multiple_of(ci * LANES, LANES), LANES)
        acc = jnp.zeros((R, LANES), F32) + b_ref[:, cs]
        for s in range(SUBLANES):
            ms = [m for m in range(base, base + n_tap) if m % SUBLANES == s]
            if not ms:
                continue
            n = R if s == 0 else R + SUBLANES
            part = None
            for m in ms:
                term = w_ref[m - base:m - base + 1, cs] * ext_ref[m - s:m - s + n, cs]
                part = term if part is None else part + term
            acc = acc + part[s:s + R]
        acc_ref[:, cs] = acc
        return carry

    lax.fori_loop(0, D // LANES, strip, 0)
    ext_ref[0:halo, :] = ext_ref[R:R + halo, :]
    o_ref[0] = _ln_silu(acc_ref[...], g_ref[...], bl_ref[...]).astype(o_ref.dtype)


def _glu_conv_prompt(u3, w_dw, b_dw, g_ln, b_ln):
    B, T, D = u3.shape
    R = _tile(T, CONV_ROWS)
    n_tap = w_dw.shape[0]
    halo = -(-(n_tap - 1) // SUBLANES) * SUBLANES
    assert R >= halo
    vec = pl.BlockSpec((1, D), lambda b, t: (0, 0))
    return pl.pallas_call(
        functools.partial(_glu_conv_kernel, rows=R, halo=halo),
        out_shape=jax.ShapeDtypeStruct((B, T, D), BF16),
        grid=(B, T // R),
        in_specs=[pl.BlockSpec((1, R, D), lambda b, t: (b, t, 0)),
                  pl.BlockSpec((n_tap, D), lambda b, t: (0, 0)), vec, vec, vec],
        out_specs=pl.BlockSpec((1, R, D), lambda b, t: (b, t, 0)),
        scratch_shapes=[pltpu.VMEM((halo + R, D), F32), pltpu.VMEM((R, D), F32)],
        compiler_params=_params("parallel", "arbitrary"),
        name="glu_conv_prompt",
    )(u3, w_dw, b_dw.reshape(1, D), g_ln.reshape(1, D), b_ln.reshape(1, D))


def _glu_conv_step_kernel(u_ref, st_ref, w_ref, b_ref, g_ref, bl_ref, o_ref, nst_ref):
    n_hist = st_ref.shape[0]
    u = u_ref[...]
    acc = b_ref[...] + w_ref[n_hist:n_hist + 1, :] * u
    for j in range(n_hist):
        acc = acc + w_ref[j:j + 1, :] * st_ref[j]
    for j in range(n_hist - 1):
        nst_ref[j] = st_ref[j + 1]
    nst_ref[n_hist - 1] = u
    o_ref[...] = _ln_silu(acc, g_ref[...], bl_ref[...]).astype(o_ref.dtype)


def _glu_conv_step(u, state_t, w_dw, b_dw, g_ln, b_ln):
    B, D = u.shape
    n_hist = state_t.shape[0]
    SB = _tile(B, 64)
    vec = pl.BlockSpec((1, D), lambda i: (0, 0))
    return pl.pallas_call(
        _glu_conv_step_kernel,
        out_shape=(jax.ShapeDtypeStruct((B, D), BF16), jax.ShapeDtypeStruct(state_t.shape, F32)),
        grid=(B // SB,),
        in_specs=[pl.BlockSpec((SB, D), lambda i: (i, 0)), pl.BlockSpec((n_hist, SB, D), lambda i: (0, i, 0)),
                  pl.BlockSpec(w_dw.shape, lambda i: (0, 0)), vec, vec, vec],
        out_specs=(pl.BlockSpec((SB, D), lambda i: (i, 0)), pl.BlockSpec((n_hist, SB, D), lambda i: (0, i, 0))),
        compiler_params=_params("parallel"),
        name="glu_conv_step",
    )(u, state_t, w_dw, b_dw.reshape(1, D), g_ln.reshape(1, D), b_ln.reshape(1, D))


def _merge_out_kernel(cv_ref, dn_ref, mg_ref, x_ref, gt_ref, wpw_ref, bpw_ref, wout_ref, g_ref, o_ref,
                      wpw_s, wout_s):
    @pl.when((pl.program_id(0) == 0) & (pl.program_id(1) == 0))
    def _():
        wpw_s[...] = wpw_ref[...].astype(BF16)
        wout_s[...] = wout_ref[...].astype(BF16)

    D = dn_ref.shape[2]
    cvo = jnp.dot(cv_ref[0], wpw_s[...], preferred_element_type=F32) + bpw_ref[...]
    mg = mg_ref[0]
    merged = mg[:, :D].astype(F32) * dn_ref[0].astype(F32) + mg[:, D:].astype(F32) * cvo
    m = jnp.dot(merged.astype(BF16), wout_s[...], preferred_element_type=F32)
    o_ref[0] = x_ref[0] + gt_ref[0] * _rms(m, g_ref[...])


def _merge_out(cv3, dn3, mg3, x3, gt, w_pw, b_pw, w_out, g_post, per_row, layer):
    G, R, D = x3.shape
    tr = _tile(R, ROW_TILE)
    row = lambda w: pl.BlockSpec((1, tr, w), lambda g_, r: (g_, r, 0))
    mat = pl.BlockSpec((None, D, D), lambda g_, r: (layer, 0, 0))
    vec = pl.BlockSpec((None, 1, D), lambda g_, r: (layer, 0, 0))
    L = w_pw.shape[0]
    return pl.pallas_call(
        _merge_out_kernel,
        out_shape=jax.ShapeDtypeStruct((G, R, D), F32),
        grid=(G, R // tr),
        in_specs=[row(D), row(D), row(2 * D), row(D), _mod_spec(per_row, tr, D), mat, vec, mat, vec],
        out_specs=row(D),
        scratch_shapes=[pltpu.VMEM((D, D), BF16), pltpu.VMEM((D, D), BF16)],
        compiler_params=_params("arbitrary", "arbitrary"),
        name="merge_out",
    )(cv3, dn3, mg3, x3, gt, w_pw, b_pw.reshape(L, 1, D), w_out, g_post.reshape(L, 1, D))


def _router_kernel(x_ref, g_ref, sc_ref, sh_ref, wr_ref, br_ref, cin_ref,
                   h_ref, idx_ref, gate_ref, rank_ref, cnt_ref, carry, *, n_experts):
    @pl.when((pl.program_id(0) == 0) & (pl.program_id(1) == 0))
    def _():
        carry[...] = cin_ref[...].astype(F32)

    h = _rms(x_ref[0], g_ref[...]) * (1.0 + sc_ref[0]) + sh_ref[0]
    tr = h.shape[0]
    _store_row_tiles(h_ref, 0, h)
    lane = lax.broadcasted_iota(I32, (tr, LANES), 1)
    logits = jnp.where(lane < n_experts, _bdot(h, wr_ref[...]) + br_ref[...], NEG_BIG)
    vals, idxs = [], []
    for _ in range(TOP_K):
        m = jnp.max(logits, axis=-1, keepdims=True)
        i = jnp.min(jnp.where(logits == m, lane, LANES), axis=-1, keepdims=True)
        vals.append(m)
        idxs.append(i)
        logits = jnp.where(lane == i, NEG_BIG, logits)
    es = [jnp.exp(v - vals[0]) for v in vals]
    tot = es[0]
    for e in es[1:]:
        tot = tot + e
    hot = [lane == i for i in idxs]
    hot_all = hot[0]
    for m in hot[1:]:
        hot_all = hot_all | m
    hot_all = hot_all.astype(BF16)
    r_i = lax.broadcasted_iota(I32, (tr, tr), 0)
    c_i = lax.broadcasted_iota(I32, (tr, tr), 1)
    before = jnp.dot((r_i > c_i).astype(BF16), hot_all, preferred_element_type=F32) + carry[...]
    ranks = [jnp.sum(jnp.where(m, before, 0.0), axis=-1, keepdims=True).astype(I32) for m in hot]
    carry[...] = carry[...] + jnp.sum(hot_all.astype(F32), axis=0, keepdims=True)
    cnt_ref[...] = carry[...].astype(I32)
    idx_out = jnp.zeros((tr, LANES), I32)
    rank_out = jnp.zeros((tr, LANES), I32)
    gate_out = jnp.zeros((tr, LANES), F32)
    for k in range(TOP_K):
        idx_out = jnp.where(lane == k, idxs[k], idx_out)
        rank_out = jnp.where(lane == k, ranks[k], rank_out)
        gate_out = jnp.where(lane == k, es[k] / tot, gate_out)
    idx_ref[0] = idx_out
    rank_ref[0] = rank_out
    gate_ref[0] = gate_out


def _router(x3, g, sc, sh, w_router, b_router, counts_in, per_row):
    G, R, D = x3.shape
    E = w_router.shape[1]
    assert E <= LANES
    tr = _tile(R, ROW_TILE)
    wr = jnp.pad(w_router, ((0, 0), (0, LANES - E)))
    br = jnp.pad(b_router, (0, LANES - E)).reshape(1, LANES)
    row = lambda w: pl.BlockSpec((1, tr, w), lambda g_, r: (g_, r, 0))
    full = lambda a: pl.BlockSpec(a.shape, lambda g_, r: (0,) * a.ndim)
    g = g.reshape(1, D)
    return pl.pallas_call(
        functools.partial(_router_kernel, n_experts=E),
        out_shape=(jax.ShapeDtypeStruct((G * R * SUBLANES, LANES), F32),
                   jax.ShapeDtypeStruct((G, R, LANES), I32), jax.ShapeDtypeStruct((G, R, LANES), F32),
                   jax.ShapeDtypeStruct((G, R, LANES), I32), jax.ShapeDtypeStruct((1, LANES), I32)),
        grid=(G, R // tr),
        in_specs=[row(D), full(g), _mod_spec(per_row, tr, D), _mod_spec(per_row, tr, D), full(wr), full(br),
                  full(counts_in)],
        out_specs=(pl.BlockSpec((tr * SUBLANES, LANES), lambda g_, r: (g_ * (R // tr) + r, 0)),
                   row(LANES), row(LANES), row(LANES), pl.BlockSpec((1, LANES), lambda g_, r: (0, 0))),
        scratch_shapes=[pltpu.VMEM((1, LANES), F32)],
        compiler_params=_params("arbitrary", "arbitrary"),
        name="router",
    )(x3, g, sc, sh, wr, br, counts_in)


def _dispatch_kernel(*refs, tokens, tile_starts):
    n_groups = len(tile_starts) - 1
    pos_ref, h_refs = refs[0], refs[1:1 + n_groups]
    x_hbm, sem = refs[1 + n_groups], refs[2 + n_groups]
    i = pl.program_id(0)

    def row_copy(h_ref, k, t):
        return pltpu.make_async_copy(_row_tile(h_ref, t), _row_tile(x_hbm, pos_ref[0, k, t]), sem.at[0])

    for gi, h_ref in enumerate(h_refs):
        @pl.when((i >= tile_starts[gi]) & (i < tile_starts[gi + 1]))
        def _(h_ref=h_ref):
            def body(t, carry):
                for k in range(TOP_K):
                    row_copy(h_ref, k, t).start()
                return carry

            lax.fori_loop(0, tokens, body, 0, unroll=DMA_UNROLL // TOP_K)
            for _ in range(TOP_K * tokens):
                row_copy(h_ref, 0, 0).wait()


def _dispatch(hs, pos_tiles, n_rows):
    TT = TOKEN_TILE
    blk = TT * SUBLANES
    tile_starts = [0]
    for h in hs:
        assert h.shape[0] % blk == 0
        tile_starts.append(tile_starts[-1] + h.shape[0] // blk)
    specs = [pl.BlockSpec((1, SUBLANES, TT), lambda i: (i, 0, 0), memory_space=pltpu.SMEM)]
    for gi, h in enumerate(hs):
        lo, n = tile_starts[gi], h.shape[0] // blk
        specs.append(pl.BlockSpec((blk, LANES), lambda i, lo=lo, n=n: (jnp.clip(i - lo, 0, n - 1), 0)))
    return pl.pallas_call(
        functools.partial(_dispatch_kernel, tokens=TT, tile_starts=tuple(tile_starts)),
        out_shape=jax.ShapeDtypeStruct((n_rows * SUBLANES, LANES), F32),
        grid=(tile_starts[-1],),
        in_specs=specs,
        out_specs=pl.BlockSpec(memory_space=pl.ANY),
        scratch_shapes=[pltpu.SemaphoreType.DMA((1,))],
        compiler_params=_params("arbitrary"),
        name="dispatch",
    )(pos_tiles, *hs)


def _expert_kernel(ib_ref, ie_ref, lo_ref, hi_ref, x_ref, wgu_ref, bgu_ref, wd_ref, bd_ref, y_ref, wgu_s, wd_s):
    i = pl.program_id(0)
    prev = jnp.maximum(i - 1, 0)
    lo, hi = lo_ref[i], hi_ref[i]
    first = (i == 0) | (ib_ref[i] != ib_ref[prev])

    @pl.when(hi > lo)
    def _():
        @pl.when((i == 0) | (ie_ref[i] != ie_ref[prev]))
        def _():
            wgu_s[...] = wgu_ref[...].astype(BF16)
            wd_s[...] = wd_ref[...].astype(BF16)

        F = wd_s.shape[0]
        rows = x_ref.shape[0] // SUBLANES
        x = _load_row_tiles(x_ref, 0, rows).astype(BF16)
        gu = jnp.dot(x, wgu_s[...], preferred_element_type=F32) + bgu_ref[...]
        gl = jnp.minimum(gu[:, :F], SWIGLU_LIMIT)
        up = jnp.clip(gu[:, F:], -SWIGLU_LIMIT, SWIGLU_LIMIT)
        act = gl * _sigmoid(SWIGLU_ALPHA * gl) * (up + 1.0)
        y = jnp.dot(act.astype(BF16), wd_s[...], preferred_element_type=F32) + bd_ref[...]
        rid = lax.broadcasted_iota(I32, (rows, 1), 0)
        mine = (rid >= lo) & (rid < hi)

        @pl.when(first)
        def _():
            _store_row_tiles(y_ref, 0, jnp.where(mine, y, 0.0))

        @pl.when(jnp.logical_not(first))
        def _():
            _store_row_tiles(y_ref, 0, jnp.where(mine, y, _load_row_tiles(y_ref, 0, rows)))


def _experts(x_sorted, items, w_gate_up, b_gate_up, w_down, b_down, layer):
    L, E, D, F2 = w_gate_up.shape
    F = w_down.shape[2]
    assert D == SUBLANES * LANES
    rows = MOE_ROWS
    blk = rows * SUBLANES
    n_items = items[0].shape[0]
    grid_spec = pltpu.PrefetchScalarGridSpec(
        num_scalar_prefetch=4,
        grid=(n_items,),
        in_specs=[
            pl.BlockSpec((blk, LANES), lambda i, ib, ie, lo, hi: (ib[i], 0)),
            pl.BlockSpec((None, None, D, F2), lambda i, ib, ie, lo, hi: (layer, ie[i], 0, 0)),
            pl.BlockSpec((None, None, 1, F2), lambda i, ib, ie, lo, hi: (layer, ie[i], 0, 0)),
            pl.BlockSpec((None, None, F, D), lambda i, ib, ie, lo, hi: (layer, ie[i], 0, 0)),
            pl.BlockSpec((None, None, 1, D), lambda i, ib, ie, lo, hi: (layer, ie[i], 0, 0)),
        ],
        out_specs=pl.BlockSpec((blk, LANES), lambda i, ib, ie, lo, hi: (ib[i], 0)),
        scratch_shapes=[pltpu.VMEM((D, F2), BF16), pltpu.VMEM((F, D), BF16)],
    )
    return pl.pallas_call(
        _expert_kernel,
        out_shape=jax.ShapeDtypeStruct(x_sorted.shape, F32),
        grid_spec=grid_spec,
        compiler_params=_params("arbitrary"),
        name="experts",
    )(*items, x_sorted, w_gate_up, b_gate_up.reshape(L, E, 1, F2), w_down, b_down.reshape(L, E, 1, D))


def _combine_kernel(pos0_ref, pos1_ref, y_hbm, gate_ref, x_ref, gt_ref, g_ref, o_ref, ybuf, sem, *, tokens):
    TT = tokens
    n_rows = TOP_K * TT
    i = pl.program_id(0) * pl.num_programs(1) + pl.program_id(1)
    n = pl.num_programs(0) * pl.num_programs(1)
    slot = lax.rem(i, 2)

    def row_copy(pos_ref, k, t, sl):
        return pltpu.make_async_copy(_row_tile(y_hbm, pos_ref[0, k, t]), _row_tile(ybuf, sl * n_rows + k * TT + t),
                                     sem.at[sl])

    def start_rows(pos_ref, sl):
        def body(t, carry):
            for k in range(TOP_K):
                row_copy(pos_ref, k, t, sl).start()
            return carry
        lax.fori_loop(0, TT, body, 0, unroll=DMA_UNROLL // TOP_K)

    @pl.when(i == 0)
    def _():
        start_rows(pos0_ref, 0)

    @pl.when(i + 1 < n)
    def _():
        start_rows(pos1_ref, 1 - slot)

    for _ in range(n_rows):
        row_copy(pos0_ref, 0, 0, slot).wait()

    gates = gate_ref[0]
    f = gates[:, 0:1] * _load_row_tiles(ybuf, slot * n_rows, TT)
    for k in range(1, TOP_K):
        f = f + gates[:, k:k + 1] * _load_row_tiles(ybuf, slot * n_rows + k * TT, TT)
    o_ref[0] = x_ref[0] + gt_ref[0] * _rms(f, g_ref[...])


def _combine(y_sorted, pos_tiles, gates3, x3, gt, g_post, per_row):
    G, R, D = x3.shape
    TT = TOKEN_TILE
    assert R % TT == 0
    nr = R // TT
    n_tiles = G * nr
    row = lambda w: pl.BlockSpec((1, TT, w), lambda g_, r: (g_, r, 0))
    cur = pl.BlockSpec((1, SUBLANES, TT), lambda g_, r: (g_ * nr + r, 0, 0), memory_space=pltpu.SMEM)
    nxt = pl.BlockSpec((1, SUBLANES, TT), lambda g_, r: (jnp.minimum(g_ * nr + r + 1, n_tiles - 1), 0, 0),
                       memory_space=pltpu.SMEM)
    return pl.pallas_call(
        functools.partial(_combine_kernel, tokens=TT),
        out_shape=jax.ShapeDtypeStruct((G, R, D), F32),
        grid=(G, nr),
        in_specs=[cur, nxt, pl.BlockSpec(memory_space=pl.ANY),
                  row(LANES), row(D), _mod_spec(per_row, TT, D), pl.BlockSpec((1, D), lambda g_, r: (0, 0))],
        out_specs=row(D),
        scratch_shapes=[pltpu.VMEM((2 * TOP_K * TT * SUBLANES, LANES), F32), pltpu.SemaphoreType.DMA((2,))],
        compiler_params=_params("arbitrary", "arbitrary"),
        name="combine",
    )(pos_tiles, pos_tiles, y_sorted, gates3, x3, gt, g_post.reshape(1, D))


def _dest_kernel(idx_ref, rank_ref, start_ref, o_ref):
    idx, rank = idx_ref[0], rank_ref[0].astype(F32)
    start = start_ref[...].astype(F32)
    lane = lax.broadcasted_iota(I32, idx.shape, 1)
    dest = jnp.zeros(idx.shape, F32)
    for k in range(TOP_K):
        first = jnp.sum(jnp.where(lane == idx[:, k:k + 1], start, 0.0), axis=1, keepdims=True)
        dest = jnp.where(lane == k, first + rank[:, k:k + 1], dest)
    o_ref[0] = dest.T[0:SUBLANES, :].astype(I32)


def _dest(idx3, rank3, start_vec):
    G, R, _ = idx3.shape
    TT = TOKEN_TILE
    assert TT == LANES and R % TT == 0 and TOP_K <= SUBLANES
    nr = R // TT
    blk = pl.BlockSpec((1, TT, LANES), lambda g_, r: (g_, r, 0))
    return pl.pallas_call(
        _dest_kernel,
        out_shape=jax.ShapeDtypeStruct((G * nr, SUBLANES, TT), I32),
        grid=(G, nr),
        in_specs=[blk, blk, pl.BlockSpec((1, LANES), lambda g_, r: (0, 0))],
        out_specs=pl.BlockSpec((1, SUBLANES, TT), lambda g_, r: (g_ * nr + r, 0, 0)),
        compiler_params=_params("parallel", "parallel"),
        name="dest",
    )(idx3, rank3, start_vec)


def _prefix_sums(v):
    n = v.shape[0]
    keep = jnp.arange(n, dtype=I32)[:, None] >= jnp.arange(n, dtype=I32)[None, :]
    return jnp.sum(jnp.where(keep, v[None, :], 0), axis=1)


def _work_items(counts, n_rows):
    E = counts.shape[0]
    rows = MOE_ROWS
    n_blocks = n_rows // rows
    n_items = n_blocks + E - 1
    end = _prefix_sums(counts)
    start = end - counts
    first_blk = start // rows
    last_blk = jnp.maximum(end - 1, 0) // rows
    per_e = jnp.where(counts > 0, last_blk - first_blk + 1, 0)
    item_end = _prefix_sums(per_e)
    item_start = item_end - per_e
    n_real = item_end[-1]
    i = jnp.clip(jnp.arange(n_items, dtype=I32), 0, jnp.maximum(n_real - 1, 0))
    e = jnp.minimum(jnp.sum((item_end[None, :] <= i[:, None]).astype(I32), axis=1), E - 1)
    b = (first_blk[e] + i - item_start[e]).astype(I32)
    lo = jnp.maximum(start[e], b * rows) - b * rows
    hi = jnp.minimum(end[e], (b + 1) * rows) - b * rows
    hi = jnp.where(jnp.arange(n_items, dtype=I32) < n_real, hi, lo)
    return b, e, lo.astype(I32), hi.astype(I32)


def _moe(groups, p, l):
    E = p["w_router"].shape[2]
    D = groups[0]["x3"].shape[2]
    n_rows = TOP_K * sum(g["x3"].shape[0] * g["x3"].shape[1] for g in groups)
    assert n_rows % MOE_ROWS == 0
    counts = jnp.zeros((1, LANES), I32)
    routed = []
    for g in groups:
        h3, idx3, gates3, rank3, counts = _router(g["x3"], p["g_pre_ffn"][l], g["sc2"], g["sh2"],
                                                  p["w_router"][l], p["b_router"][l], counts, g["per_row"])
        routed.append((h3, idx3, gates3, rank3))
    counts = counts[0, :E]
    start = _prefix_sums(counts) - counts
    start_vec = jnp.pad(start, (0, LANES - E)).reshape(1, LANES)
    tiles = [_dest(idx3, rank3, start_vec) for _, idx3, _, rank3 in routed]
    x_sorted = _dispatch([r[0] for r in routed], jnp.concatenate(tiles, axis=0), n_rows)
    y_sorted = _experts(x_sorted, _work_items(counts, n_rows), p["w_gate_up"], p["b_gate_up"], p["w_down"],
                        p["b_down"], l)
    return [_combine(y_sorted, t, r[2], g["x3"], g["gt2"], p["g_post_ffn"][l], g["per_row"])
            for g, r, t in zip(groups, routed, tiles)]


def _mix(grp, p, l, w_tail, w_ba):
    x3, c, per_row = grp["x3"], grp["c"], grp["per_row"]
    B, T = grp["B"], grp["T"]
    G, R, D = x3.shape
    N = G * R
    H = p["a_log"].shape[1]
    KW = H * HEAD_DIM
    QKV = 3 * KW
    mod = (lambda a: a.reshape(1, B, D)) if per_row else (lambda a: a.reshape(B, 1, D))
    ada = _mm(c, p["w_ada"], F32, layer=l, n=6 * D, b=p["b_ada"], in_silu=True, name="ada")
    sh1, sc1, gt1, sh2, sc2, gt2 = (mod(ada[:, i * D:(i + 1) * D]) for i in range(6))
    h = _norm_mod(x3, p["g_pre_mix"][l], sc1, sh1, per_row).reshape(N, D)
    qkv = _mm(h, p["w_in"], F32, layer=l, n=QKV, name="proj_qkv")
    zs = _mm(h, p["w_in"], F32, layer=l, n=KW, col=QKV, act="silu", name="proj_z")
    ba = _mm(h, w_ba, F32, layer=l, n=LANES, name="proj_ba")
    u = _mm(h, w_tail, F32, layer=l, n=D, col=0, col2=D, name="proj_glu")
    mg = _mm(h, w_tail, F32, layer=l, n=2 * D, col=2 * D, act="sigmoid", name="proj_gate")
    alog_vec = jnp.pad(p["a_log"][l], (H, LANES - 2 * H)).reshape(1, LANES)
    dt_vec = jnp.pad(p["dt_bias"][l], (H, LANES - 2 * H)).reshape(1, LANES)
    conv_w = (p["w_dw_conv"][l], p["b_dw_conv"][l], p["g_ln_conv"][l], p["b_ln_conv"][l])
    if per_row:
        st_q = jnp.swapaxes(grp["buf_qkv"][l], 0, 1)
        dn, nst_q, s_new = _delta_step(qkv, ba, zs, st_q, grp["s_dn"][l], p["w_qkv_conv"][l], alog_vec, dt_vec,
                                       p["g_dn_norm"][l], H)
        bq = jnp.swapaxes(nst_q, 0, 1)
        cv, nst_g = _glu_conv_step(u, jnp.swapaxes(grp["buf_glu"][l], 0, 1), *conv_w)
        bg = jnp.swapaxes(nst_g, 0, 1)
    else:
        dn, s_new = _delta_prompt(qkv.reshape(B, T, QKV), ba.reshape(B, T, LANES), zs.reshape(B, T, KW),
                                  p["w_qkv_conv"][l], alog_vec, dt_vec, p["g_dn_norm"][l], H)
        bq = qkv.reshape(B, T, QKV)[:, T - (p["w_qkv_conv"].shape[1] - 1):, :]
        cv = _glu_conv_prompt(u.reshape(B, T, D), *conv_w)
        bg = u.reshape(B, T, D)[:, T - (p["w_dw_conv"].shape[1] - 1):, :]
    x3 = _merge_out(cv.reshape(G, R, D), dn.reshape(G, R, KW), mg.reshape(G, R, 2 * D), x3, gt1,
                    p["w_pw_conv"], p["b_pw_conv"], p["w_out"], p["g_post_mix"], per_row, l)
    return dict(grp, x3=x3, sc2=sc2, sh2=sh2, gt2=gt2), (s_new, bq, bg)


def kernel(x_prompt, x_sample, c_prompt, c_sample, state_dn, state_qkv_conv, state_glu_conv, w_ada, b_ada, g_pre_mix, g_post_mix, g_pre_ffn, g_post_ffn, w_in, w_qkv_conv, a_log, dt_bias, g_dn_norm, w_dw_conv, b_dw_conv, g_ln_conv, b_ln_conv, w_pw_conv, b_pw_conv, w_out, w_router, b_router, w_gate_up, b_gate_up, w_down, b_down):
    p = dict(w_ada=w_ada, b_ada=b_ada, g_pre_mix=g_pre_mix, g_post_mix=g_post_mix, g_pre_ffn=g_pre_ffn,
             g_post_ffn=g_post_ffn, w_in=w_in, w_qkv_conv=w_qkv_conv, a_log=a_log, dt_bias=dt_bias,
             g_dn_norm=g_dn_norm, w_dw_conv=w_dw_conv, b_dw_conv=b_dw_conv, g_ln_conv=g_ln_conv,
             b_ln_conv=b_ln_conv, w_pw_conv=w_pw_conv, b_pw_conv=b_pw_conv, w_out=w_out,
             w_router=w_router, b_router=b_router, w_gate_up=w_gate_up, b_gate_up=b_gate_up,
             w_down=w_down, b_down=b_down)
    depth, D = w_in.shape[0], w_in.shape[1]
    H = a_log.shape[1]
    o_ba = 4 * H * HEAD_DIM
    w_ba = jnp.pad(w_in[:, :, o_ba:o_ba + 2 * H], ((0, 0), (0, 0), (0, LANES - 2 * H)))
    w_tail = w_in[:, :, o_ba + 2 * H:]
    Bp, Tp, _ = x_prompt.shape
    Bs, Ts, _ = x_sample.shape
    assert Tp > 1 and Ts == 1
    groups = [dict(x3=x_prompt, c=c_prompt, per_row=False, B=Bp, T=Tp),
              dict(x3=x_sample.reshape(1, Bs, D), c=c_sample, per_row=True, B=Bs, T=Ts,
                   s_dn=state_dn, buf_qkv=state_qkv_conv, buf_glu=state_glu_conv)]
    states = [[], []]
    for l in range(depth):
        mixed = []
        for gi, grp in enumerate(groups):
            grp, st = _mix(grp, p, l, w_tail, w_ba)
            mixed.append(grp)
            states[gi].append(st)
        outs = _moe(mixed, p, l)
        groups = [dict(grp, x3=x3) for grp, x3 in zip(mixed, outs)]
    stack = lambda gi, j, like: jnp.stack([st[j] for st in states[gi]]).astype(like.dtype)
    return (groups[0]["x3"], groups[1]["x3"].reshape(Bs, Ts, D),
            stack(0, 0, state_dn), stack(0, 1, state_qkv_conv), stack(0, 2, state_glu_conv),
            stack(1, 0, state_dn), stack(1, 1, state_qkv_conv), stack(1, 2, state_glu_conv))
```

```python
import functools

import jax
import jax.numpy as jnp
from jax import lax
from jax.experimental import pallas as pl
from jax.experimental.pallas import tpu as pltpu

F32, BF16, I32 = jnp.float32, jnp.bfloat16, jnp.int32

NORM_EPS = 1e-6
HEAD_DIM = 128
TOP_K = 4
SWIGLU_LIMIT = 7.0
SWIGLU_ALPHA = 1.702

LANES = 128
SUBLANES = 8
VMEM_LIMIT_BYTES = 56 * 1024 * 1024

DELTA_CHUNK = 64
DELTA_SEQS = 2
CONV_ROWS = 128
MM_ROWS = 2048
MM_COLS = 512
ROW_TILE = 512
MOE_ROWS = 256
TOKEN_TILE = 128
DMA_UNROLL = 8
DECODE_SEQS = 8

NEG_BIG = -1e30


def _params(*sem):
    return pltpu.CompilerParams(dimension_semantics=sem, vmem_limit_bytes=VMEM_LIMIT_BYTES)


def _sigmoid(x):
    return 1.0 / (1.0 + jnp.exp(-x))


def _silu(x):
    return x * _sigmoid(x)


def _softplus(x):
    return jnp.maximum(x, 0.0) + jnp.log(1.0 + jnp.exp(-jnp.abs(x)))


def _rms(x, g):
    return x * lax.rsqrt(jnp.mean(x * x, axis=-1, keepdims=True) + NORM_EPS) * g


def _bdot(a, b):
    return jnp.dot(a.astype(BF16), b.astype(BF16), preferred_element_type=F32)


def _bdot_nt(a, b):
    return lax.dot_general(a.astype(BF16), b.astype(BF16), (((1,), (1,)), ((), ())),
                           preferred_element_type=F32)


def _fdot(a, b):
    return jnp.dot(a, b, preferred_element_type=F32, precision=lax.Precision.HIGHEST)


def _tile(n, t):
    t = min(n, t)
    assert n % t == 0, (n, t)
    return t


def _mod_spec(per_row, tr, d):
    return pl.BlockSpec((1, tr if per_row else 1, d), (lambda g, r: (g, r, 0)) if per_row else (lambda g, r: (g, 0, 0)))


def _load_row_tiles(ref, start, rows):
    return jnp.concatenate([ref[pl.ds(start * SUBLANES + c, rows, stride=SUBLANES), :] for c in range(SUBLANES)],
                           axis=1)


def _store_row_tiles(ref, start, x):
    for c in range(SUBLANES):
        ref[pl.ds(start * SUBLANES + c, x.shape[0], stride=SUBLANES), :] = x[:, c * LANES:(c + 1) * LANES]


def _row_tile(ref, r):
    return ref.at[pl.ds(pl.multiple_of(r * SUBLANES, SUBLANES), SUBLANES)]


def _mm_kernel(*refs, in_silu, act, has_bias, glu):
    x_ref, w_ref = refs[0], refs[1]
    pos = 2
    w2_ref = None
    if glu:
        w2_ref = refs[pos]
        pos += 1
    b_ref = None
    if has_bias:
        b_ref = refs[pos]
        pos += 1
    o_ref = refs[pos]
    x = x_ref[...]
    if in_silu:
        x = _silu(x.astype(F32))
    x = x.astype(BF16)
    y = _bdot(x, w_ref[...])
    if has_bias:
        y = y + b_ref[...]
    if glu:
        y = y * _sigmoid(_bdot(x, w2_ref[...]))
    if act == "silu":
        y = _silu(y)
    elif act == "sigmoid":
        y = _sigmoid(y)
    o_ref[...] = y.astype(o_ref.dtype)


def _mm(x, w, out_dtype, *, layer, n, col=0, col2=None, b=None, act=None, in_silu=False, name="mm"):
    m, k = x.shape
    tm, tn = _tile(m, MM_ROWS), _tile(n, MM_COLS)
    assert col % tn == 0 and (col2 is None or col2 % tn == 0)
    wspec = lambda c0: pl.BlockSpec((None, k, tn), lambda i, j: (layer, 0, j + c0 // tn))
    ins = [x, w]
    specs = [pl.BlockSpec((tm, k), lambda i, j: (i, 0)), wspec(col)]
    if col2 is not None:
        ins.append(w)
        specs.append(wspec(col2))
    if b is not None:
        ins.append(b.reshape(b.shape[0], 1, b.shape[1]))
        specs.append(pl.BlockSpec((None, 1, tn), lambda i, j: (layer, 0, j + col // tn)))
    return pl.pallas_call(
        functools.partial(_mm_kernel, in_silu=in_silu, act=act, has_bias=b is not None, glu=col2 is not None),
        out_shape=jax.ShapeDtypeStruct((m, n), out_dtype),
        grid=(m // tm, n // tn),
        in_specs=specs,
        out_specs=pl.BlockSpec((tm, tn), lambda i, j: (i, j)),
        compiler_params=_params("parallel", "arbitrary"),
        name=name,
    )(*ins)


def _norm_mod_kernel(x_ref, g_ref, sc_ref, sh_ref, o_ref):
    y = _rms(x_ref[0], g_ref[...])
    o_ref[0] = (y * (1.0 + sc_ref[0]) + sh_ref[0]).astype(o_ref.dtype)


def _norm_mod(x3, g, sc, sh, per_row):
    G, R, D = x3.shape
    tr = _tile(R, ROW_TILE)
    row = pl.BlockSpec((1, tr, D), lambda g_, r: (g_, r, 0))
    return pl.pallas_call(
        _norm_mod_kernel,
        out_shape=jax.ShapeDtypeStruct((G, R, D), BF16),
        grid=(G, R // tr),
        in_specs=[row, pl.BlockSpec((1, D), lambda g_, r: (0, 0)), _mod_spec(per_row, tr, D), _mod_spec(per_row, tr, D)],
        out_specs=row,
        compiler_params=_params("parallel", "parallel"),
        name="norm_mod",
    )(x3, g.reshape(1, D), sc, sh)


def _delta_kernel(q_ref, k_ref, v_ref, wq_ref, wk_ref, wv_ref, ba_ref, alog_ref, dt_ref, zs_ref, gn_ref,
                  dn_ref, s_ref, ext_ref, *, heads_per_step, n_heads, chunk):
    C, HB, H = chunk, heads_per_step, n_heads
    BB = q_ref.shape[0]
    hb = pl.program_id(1)
    c = pl.program_id(2)

    @pl.when(c == 0)
    def _():
        s_ref[...] = jnp.zeros_like(s_ref)
        ext_ref[:, 0:SUBLANES, :] = jnp.zeros((3 * BB, SUBLANES, HB * HEAD_DIM), F32)

    def conv(bb, i, x_ref, w_ref):
        e = 3 * bb + i
        ext_ref[e, SUBLANES:SUBLANES + C, :] = x_ref[bb]
        w = w_ref[...]
        n_tap = w.shape[0]
        acc = w[n_tap - 1:n_tap] * ext_ref[e, SUBLANES:SUBLANES + C, :]
        for j in range(n_tap - 1):
            lo = SUBLANES - (n_tap - 1) + j
            acc = acc + w[j:j + 1] * ext_ref[e, lo:lo + C, :]
        ext_ref[e, SUBLANES - (n_tap - 1):SUBLANES, :] = ext_ref[e, SUBLANES + C - (n_tap - 1):SUBLANES + C, :]
        return _silu(acc)

    q = [conv(bb, 0, q_ref, wq_ref) for bb in range(BB)]
    k = [conv(bb, 1, k_ref, wk_ref) for bb in range(BB)]
    v = [conv(bb, 2, v_ref, wv_ref) for bb in range(BB)]

    row = lax.broadcasted_iota(I32, (C, C), 0)
    col = lax.broadcasted_iota(I32, (C, C), 1)
    ge = row >= col
    gt = row > col
    ba = [ba_ref[bb] for bb in range(BB)]
    beta_all = [_sigmoid(x) for x in ba]
    gc_all = [_fdot(ge.astype(F32), -jnp.exp(alog_ref[...]) * _softplus(x + dt_ref[...])) for x in ba]
    gc_t = [x.T for x in gc_all]
    lane = lax.broadcasted_iota(I32, (C, LANES), 1)
    sub = lax.broadcasted_iota(I32, (LANES, C), 0)
    gn = gn_ref[...]

    units = [(bb, hh) for bb in range(BB) for hh in range(HB)]
    hs = range(len(units))
    sls = [slice(hh * HEAD_DIM, (hh + 1) * HEAD_DIM) for _, hh in units]
    beta = [jnp.sum(jnp.where(lane == hb * HB + hh, beta_all[bb], 0.0), axis=1, keepdims=True)
            for bb, hh in units]
    gc = [jnp.sum(jnp.where(lane == H + hb * HB + hh, gc_all[bb], 0.0), axis=1, keepdims=True)
          for bb, hh in units]
    gc_row = [jnp.sum(jnp.where(sub == H + hb * HB + hh, gc_t[bb], 0.0), axis=0, keepdims=True)
              for bb, hh in units]
    qh = [q[bb][:, sls[u]] for u, (bb, _) in enumerate(units)]
    kh = [k[bb][:, sls[u]] for u, (bb, _) in enumerate(units)]
    vh = [v[bb][:, sls[u]] for u, (bb, _) in enumerate(units)]
    qh = [x * lax.rsqrt(jnp.sum(x * x, axis=-1, keepdims=True) + NORM_EPS) * (HEAD_DIM ** -0.5) for x in qh]
    kh = [x * lax.rsqrt(jnp.sum(x * x, axis=-1, keepdims=True) + NORM_EPS) for x in kh]
    eg = [jnp.exp(x) for x in gc]
    kb = [kh[hh] * beta[hh] for hh in hs]
    decay = [jnp.where(ge, jnp.exp(jnp.where(ge, gc[hh] - gc_row[hh], 0.0)), 0.0) for hh in hs]
    kk = [_bdot_nt(kb[hh], kh[hh]) for hh in hs]
    qk = [_bdot_nt(qh[hh], kh[hh]) for hh in hs]
    low = [jnp.where(gt, kk[hh] * decay[hh], 0.0) for hh in hs]
    rhs = [jnp.concatenate([vh[hh] * beta[hh], kb[hh] * eg[hh]], axis=1) for hh in hs]
    sol = [rhs[hh] - _bdot(low[hh], rhs[hh]) for hh in hs]
    pw = low
    span = 2
    while span < C:
        pw = [_bdot(x, x) for x in pw]
        sol = [sol[hh] + _bdot(pw[hh], sol[hh]) for hh in hs]
        span *= 2
    s = [s_ref[bb, hh] for bb, hh in units]
    r = [_bdot(jnp.concatenate([sol[hh][:, HEAD_DIM:], qh[hh] * eg[hh]], axis=0), s[hh]) for hh in hs]
    v_new = [sol[hh][:, :HEAD_DIM] - r[hh][:C] for hh in hs]
    o = [r[hh][C:] + _bdot(qk[hh] * decay[hh], v_new[hh]) for hh in hs]
    g_last = [x[C - 1:C, :] for x in gc]
    kd = [kh[hh] * jnp.exp(g_last[hh] - gc[hh]) for hh in hs]
    s_new = [s[hh] * jnp.exp(g_last[hh]) + _bdot(kd[hh].T, v_new[hh]) for hh in hs]
    for u, (bb, hh) in enumerate(units):
        s_ref[bb, hh] = s_new[u]
        on = _rms(o[u], gn) * zs_ref[bb, :, sls[u]].astype(F32)
        dn_ref[bb, :, sls[u]] = on.astype(dn_ref.dtype)


def _delta_prompt(qkv3, ba3, zs3, w_conv, alog_vec, dt_vec, g_norm, n_heads):
    B, T, W3 = qkv3.shape
    H = n_heads
    HB = H
    nHB = H // HB
    C = DELTA_CHUNK
    BB = _tile(B, DELTA_SEQS)
    assert T % C == 0 and W3 == 3 * H * HEAD_DIM
    wblk = HB * HEAD_DIM
    qspec = lambda off: pl.BlockSpec((BB, C, wblk), lambda b, h, c: (b, c, off + h))
    wspec = lambda off: pl.BlockSpec((w_conv.shape[0], wblk), lambda b, h, c: (0, off + h))
    vec = pl.BlockSpec((1, LANES), lambda b, h, c: (0, 0))
    return pl.pallas_call(
        functools.partial(_delta_kernel, heads_per_step=HB, n_heads=H, chunk=C),
        out_shape=(jax.ShapeDtypeStruct((B, T, H * HEAD_DIM), F32),
                   jax.ShapeDtypeStruct((B, H, HEAD_DIM, HEAD_DIM), F32)),
        grid=(B // BB, nHB, T // C),
        in_specs=[qspec(0), qspec(nHB), qspec(2 * nHB), wspec(0), wspec(nHB), wspec(2 * nHB),
                  pl.BlockSpec((BB, C, LANES), lambda b, h, c: (b, c, 0)), vec, vec,
                  pl.BlockSpec((BB, C, wblk), lambda b, h, c: (b, c, h)), vec],
        out_specs=(pl.BlockSpec((BB, C, wblk), lambda b, h, c: (b, c, h)),
                   pl.BlockSpec((BB, HB, HEAD_DIM, HEAD_DIM), lambda b, h, c: (b, h, 0, 0))),
        scratch_shapes=[pltpu.VMEM((3 * BB, SUBLANES + C, wblk), F32)],
        compiler_params=_params("parallel", "parallel", "arbitrary"),
        name="delta_prompt",
    )(qkv3, qkv3, qkv3, w_conv, w_conv, w_conv, ba3, alog_vec, dt_vec, zs3, g_norm.reshape(1, HEAD_DIM))


def _delta_step_kernel(x_ref, st_ref, w_ref, ba_ref, alog_ref, dt_ref, zs_ref, gn_ref, s_ref,
                       dn_ref, nst_ref, ns_ref, *, n_heads):
    H = n_heads
    SB = x_ref.shape[0]
    x = x_ref[...]
    w = w_ref[...]
    n_tap = w.shape[0]
    acc = w[n_tap - 1:n_tap] * x
    for j in range(n_tap - 1):
        acc = acc + w[j:j + 1] * st_ref[j]
    for j in range(n_tap - 2):
        nst_ref[j] = st_ref[j + 1]
    nst_ref[n_tap - 2] = x
    qkv = _silu(acc)
    ba = ba_ref[...]
    beta_all = _sigmoid(ba)
    a_all = jnp.exp(-jnp.exp(alog_ref[...]) * _softplus(ba + dt_ref[...]))
    gn = gn_ref[...]
    pad = jnp.zeros((HEAD_DIM - SB, HEAD_DIM), F32)
    KW = H * HEAD_DIM
    for hh in range(H):
        sl = slice(hh * HEAD_DIM, (hh + 1) * HEAD_DIM)
        qh, kh, vh = qkv[:, sl], qkv[:, KW + hh * HEAD_DIM:KW + (hh + 1) * HEAD_DIM], qkv[:, 2 * KW + hh * HEAD_DIM:2 * KW + (hh + 1) * HEAD_DIM]
        qh = qh * lax.rsqrt(jnp.sum(qh * qh, axis=-1, keepdims=True) + NORM_EPS) * (HEAD_DIM ** -0.5)
        kh = kh * lax.rsqrt(jnp.sum(kh * kh, axis=-1, keepdims=True) + NORM_EPS)
        q_t = jnp.concatenate([qh, pad], axis=0).T
        k_t = jnp.concatenate([kh, pad], axis=0).T
        rnd = lambda a: a.astype(BF16).astype(F32)
        q_r, k_r = rnd(q_t), rnd(k_t)
        rows = []
        for s in range(SB):
            st = s_ref[s, hh] * a_all[s:s + 1, H + hh:H + hh + 1]
            ks = jnp.sum(k_r[:, s:s + 1] * rnd(st), axis=0, keepdims=True)
            delta = (vh[s:s + 1, :] - ks) * beta_all[s:s + 1, hh:hh + 1]
            st = st + k_t[:, s:s + 1] * delta
            ns_ref[s, hh] = st
            rows.append(jnp.sum(q_r[:, s:s + 1] * rnd(st), axis=0, keepdims=True))
        o = jnp.concatenate(rows, axis=0)
        on = _rms(o, gn) * zs_ref[:, sl].astype(F32)
        dn_ref[:, sl] = on.astype(dn_ref.dtype)


def _delta_step(qkv, ba, zs, state_q_t, state_s, w_conv, alog_vec, dt_vec, g_norm, n_heads):
    B, W3 = qkv.shape
    H = n_heads
    SB = _tile(B, DECODE_SEQS)
    n_hist = state_q_t.shape[0]
    vec = pl.BlockSpec((1, LANES), lambda i: (0, 0))
    return pl.pallas_call(
        functools.partial(_delta_step_kernel, n_heads=H),
        out_shape=(jax.ShapeDtypeStruct((B, H * HEAD_DIM), F32),
                   jax.ShapeDtypeStruct(state_q_t.shape, F32),
                   jax.ShapeDtypeStruct(state_s.shape, F32)),
        grid=(B // SB,),
        in_specs=[pl.BlockSpec((SB, W3), lambda i: (i, 0)),
                  pl.BlockSpec((n_hist, SB, W3), lambda i: (0, i, 0)),
                  pl.BlockSpec(w_conv.shape, lambda i: (0, 0)),
                  pl.BlockSpec((SB, LANES), lambda i: (i, 0)), vec, vec,
                  pl.BlockSpec((SB, H * HEAD_DIM), lambda i: (i, 0)), vec,
                  pl.BlockSpec((SB, H, HEAD_DIM, HEAD_DIM), lambda i: (i, 0, 0, 0))],
        out_specs=(pl.BlockSpec((SB, H * HEAD_DIM), lambda i: (i, 0)),
                   pl.BlockSpec((n_hist, SB, W3), lambda i: (0, i, 0)),
                   pl.BlockSpec((SB, H, HEAD_DIM, HEAD_DIM), lambda i: (i, 0, 0, 0))),
        compiler_params=_params("parallel"),
        name="delta_step",
    )(qkv, state_q_t, w_conv, ba, alog_vec, dt_vec, zs, g_norm.reshape(1, HEAD_DIM), state_s)


def _ln_silu(x, g, b):
    mu = jnp.mean(x, axis=-1, keepdims=True)
    xc = x - mu
    y = xc * lax.rsqrt(jnp.mean(xc * xc, axis=-1, keepdims=True) + NORM_EPS)
    return _silu(y * g + b)


def _glu_conv_kernel(u_ref, w_ref, b_ref, g_ref, bl_ref, o_ref, ext_ref, acc_ref, *, rows, halo):
    R = rows
    n_tap = w_ref.shape[0]
    D = u_ref.shape[2]
    base = halo - (n_tap - 1)

    @pl.when(pl.program_id(1) == 0)
    def _():
        ext_ref[0:halo, :] = jnp.zeros((halo, D), F32)

    ext_ref[halo:halo + R, :] = u_ref[0]

    def strip(ci, carry):
        cs = pl.ds(pl.multiple_of(ci * LANES, LANES), LANES)
        acc = jnp.zeros((R, LANES), F32) + b_ref[:, cs]
        for s in range(SUBLANES):
            ms = [m for m in range(base, base + n_tap) if m % SUBLANES == s]
            if not ms:
                continue
            n = R if s == 0 else R + SUBLANES
            part = None
            for m in ms:
                term = w_ref[m - base:m - base + 1, cs] * ext_ref[m - s:m - s + n, cs]
                part = term if part is None else part + term
            acc = acc + part[s:s + R]
        acc_ref[:, cs] = acc
        return carry

    lax.fori_loop(0, D // LANES, strip, 0)
    ext_ref[0:halo, :] = ext_ref[R:R + halo, :]
    o_ref[0] = _ln_silu(acc_ref[...], g_ref[...], bl_ref[...]).astype(o_ref.dtype)


def _glu_conv_prompt(u3, w_dw, b_dw, g_ln, b_ln):
    B, T, D = u3.shape
    R = _tile(T, CONV_ROWS)
    n_tap = w_dw.shape[0]
    halo = -(-(n_tap - 1) // SUBLANES) * SUBLANES
    assert R >= halo
    vec = pl.BlockSpec((1, D), lambda b, t: (0, 0))
    return pl.pallas_call(
        functools.partial(_glu_conv_kernel, rows=R, halo=halo),
        out_shape=jax.ShapeDtypeStruct((B, T, D), BF16),
        grid=(B, T // R),
        in_specs=[pl.BlockSpec((1, R, D), lambda b, t: (b, t, 0)),
                  pl.BlockSpec((n_tap, D), lambda b, t: (0, 0)), vec, vec, vec],
        out_specs=pl.BlockSpec((1, R, D), lambda b, t: (b, t, 0)),
        scratch_shapes=[pltpu.VMEM((halo + R, D), F32), pltpu.VMEM((R, D), F32)],
        compiler_params=_params("parallel", "arbitrary"),
        name="glu_conv_prompt",
    )(u3, w_dw, b_dw.reshape(1, D), g_ln.reshape(1, D), b_ln.reshape(1, D))


def _glu_conv_step_kernel(u_ref, st_ref, w_ref, b_ref, g_ref, bl_ref, o_ref, nst_ref):
    n_hist = st_ref.shape[0]
    u = u_ref[...]
    acc = b_ref[...] + w_ref[n_hist:n_hist + 1, :] * u
    for j in range(n_hist):
        acc = acc + w_ref[j:j + 1, :] * st_ref[j]
    for j in range(n_hist - 1):
        nst_ref[j] = st_ref[j + 1]
    nst_ref[n_hist - 1] = u
    o_ref[...] = _ln_silu(acc, g_ref[...], bl_ref[...]).astype(o_ref.dtype)


def _glu_conv_step(u, state_t, w_dw, b_dw, g_ln, b_ln):
    B, D = u.shape
    n_hist = state_t.shape[0]
    SB = _tile(B, 64)
    vec = pl.BlockSpec((1, D), lambda i: (0, 0))
    return pl.pallas_call(
        _glu_conv_step_kernel,
        out_shape=(jax.ShapeDtypeStruct((B, D), BF16), jax.ShapeDtypeStruct(state_t.shape, F32)),
        grid=(B // SB,),
        in_specs=[pl.BlockSpec((SB, D), lambda i: (i, 0)), pl.BlockSpec((n_hist, SB, D), lambda i: (0, i, 0)),
                  pl.BlockSpec(w_dw.shape, lambda i: (0, 0)), vec, vec, vec],
        out_specs=(pl.BlockSpec((SB, D), lambda i: (i, 0)), pl.BlockSpec((n_hist, SB, D), lambda i: (0, i, 0))),
        compiler_params=_params("parallel"),
        name="glu_conv_step",
    )(u, state_t, w_dw, b_dw.reshape(1, D), g_ln.reshape(1, D), b_ln.reshape(1, D))


def _merge_out_kernel(cv_ref, dn_ref, mg_ref, x_ref, gt_ref, wpw_ref, bpw_ref, wout_ref, g_ref, o_ref,
                      wpw_s, wout_s):
    @pl.when((pl.program_id(0) == 0) & (pl.program_id(1) == 0))
    def _():
        wpw_s[...] = wpw_ref[...].astype(BF16)
        wout_s[...] = wout_ref[...].astype(BF16)

    D = dn_ref.shape[2]
    cvo = jnp.dot(cv_ref[0], wpw_s[...], preferred_element_type=F32) + bpw_ref[...]
    mg = mg_ref[0]
    merged = mg[:, :D].astype(F32) * dn_ref[0].astype(F32) + mg[:, D:].astype(F32) * cvo
    m = jnp.dot(merged.astype(BF16), wout_s[...], preferred_element_type=F32)
    o_ref[0] = x_ref[0] + gt_ref[0] * _rms(m, g_ref[...])


def _merge_out(cv3, dn3, mg3, x3, gt, w_pw, b_pw, w_out, g_post, per_row, layer):
    G, R, D = x3.shape
    tr = _tile(R, ROW_TILE)
    row = lambda w: pl.BlockSpec((1, tr, w), lambda g_, r: (g_, r, 0))
    mat = pl.BlockSpec((None, D, D), lambda g_, r: (layer, 0, 0))
    vec = pl.BlockSpec((None, 1, D), lambda g_, r: (layer, 0, 0))
    L = w_pw.shape[0]
    return pl.pallas_call(
        _merge_out_kernel,
        out_shape=jax.ShapeDtypeStruct((G, R, D), F32),
        grid=(G, R // tr),
        in_specs=[row(D), row(D), row(2 * D), row(D), _mod_spec(per_row, tr, D), mat, vec, mat, vec],
        out_specs=row(D),
        scratch_shapes=[pltpu.VMEM((D, D), BF16), pltpu.VMEM((D, D), BF16)],
        compiler_params=_params("arbitrary", "arbitrary"),
        name="merge_out",
    )(cv3, dn3, mg3, x3, gt, w_pw, b_pw.reshape(L, 1, D), w_out, g_post.reshape(L, 1, D))


def _router_kernel(x_ref, g_ref, sc_ref, sh_ref, wr_ref, br_ref, cin_ref,
                   h_ref, idx_ref, gate_ref, rank_ref, cnt_ref, carry, *, n_experts):
    @pl.when((pl.program_id(0) == 0) & (pl.program_id(1) == 0))
    def _():
        carry[...] = cin_ref[...].astype(F32)

    h = _rms(x_ref[0], g_ref[...]) * (1.0 + sc_ref[0]) + sh_ref[0]
    tr = h.shape[0]
    _store_row_tiles(h_ref, 0, h)
    lane = lax.broadcasted_iota(I32, (tr, LANES), 1)
    logits = jnp.where(lane < n_experts, _bdot(h, wr_ref[...]) + br_ref[...], NEG_BIG)
    vals, idxs = [], []
    for _ in range(TOP_K):
        m = jnp.max(logits, axis=-1, keepdims=True)
        i = jnp.min(jnp.where(logits == m, lane, LANES), axis=-1, keepdims=True)
        vals.append(m)
        idxs.append(i)
        logits = jnp.where(lane == i, NEG_BIG, logits)
    es = [jnp.exp(v - vals[0]) for v in vals]
    tot = es[0]
    for e in es[1:]:
        tot = tot + e
    hot = [lane == i for i in idxs]
    hot_all = hot[0]
    for m in hot[1:]:
        hot_all = hot_all | m
    hot_all = hot_all.astype(BF16)
    r_i = lax.broadcasted_iota(I32, (tr, tr), 0)
    c_i = lax.broadcasted_iota(I32, (tr, tr), 1)
    before = jnp.dot((r_i > c_i).astype(BF16), hot_all, preferred_element_type=F32) + carry[...]
    ranks = [jnp.sum(jnp.where(m, before, 0.0), axis=-1, keepdims=True).astype(I32) for m in hot]
    carry[...] = carry[...] + jnp.sum(hot_all.astype(F32), axis=0, keepdims=True)
    cnt_ref[...] = carry[...].astype(I32)
    idx_out = jnp.zeros((tr, LANES), I32)
    rank_out = jnp.zeros((tr, LANES), I32)
    gate_out = jnp.zeros((tr, LANES), F32)
    for k in range(TOP_K):
        idx_out = jnp.where(lane == k, idxs[k], idx_out)
        rank_out = jnp.where(lane == k, ranks[k], rank_out)
        gate_out = jnp.where(lane == k, es[k] / tot, gate_out)
    idx_ref[0] = idx_out
    rank_ref[0] = rank_out
    gate_ref[0] = gate_out


def _router(x3, g, sc, sh, w_router, b_router, counts_in, per_row):
    G, R, D = x3.shape
    E = w_router.shape[1]
    assert E <= LANES
    tr = _tile(R, ROW_TILE)
    wr = jnp.pad(w_router, ((0, 0), (0, LANES - E)))
    br = jnp.pad(b_router, (0, LANES - E)).reshape(1, LANES)
    row = lambda w: pl.BlockSpec((1, tr, w), lambda g_, r: (g_, r, 0))
    full = lambda a: pl.BlockSpec(a.shape, lambda g_, r: (0,) * a.ndim)
    g = g.reshape(1, D)
    return pl.pallas_call(
        functools.partial(_router_kernel, n_experts=E),
        out_shape=(jax.ShapeDtypeStruct((G * R * SUBLANES, LANES), F32),
                   jax.ShapeDtypeStruct((G, R, LANES), I32), jax.ShapeDtypeStruct((G, R, LANES), F32),
                   jax.ShapeDtypeStruct((G, R, LANES), I32), jax.ShapeDtypeStruct((1, LANES), I32)),
        grid=(G, R // tr),
        in_specs=[row(D), full(g), _mod_spec(per_row, tr, D), _mod_spec(per_row, tr, D), full(wr), full(br),
                  full(counts_in)],
        out_specs=(pl.BlockSpec((tr * SUBLANES, LANES), lambda g_, r: (g_ * (R // tr) + r, 0)),
                   row(LANES), row(LANES), row(LANES), pl.BlockSpec((1, LANES), lambda g_, r: (0, 0))),
        scratch_shapes=[pltpu.VMEM((1, LANES), F32)],
        compiler_params=_params("arbitrary", "arbitrary"),
        name="router",
    )(x3, g, sc, sh, wr, br, counts_in)


def _dispatch_kernel(*refs, tokens, tile_starts):
    n_groups = len(tile_starts) - 1
    pos_ref, h_refs = refs[0], refs[1:1 + n_groups]
    x_hbm, stage, sem = refs[1 + n_groups], refs[2 + n_groups], refs[3 + n_groups]
    i = pl.program_id(0)
    n = pl.num_programs(0)
    slot = lax.rem(i, 2)
    blk = tokens * SUBLANES

    def row_copy(k, t, sl):
        return pltpu.make_async_copy(_row_tile(stage, sl * tokens + t), _row_tile(x_hbm, pos_ref[0, k, t]),
                                     sem.at[sl])

    def drain(sl):
        for _ in range(TOP_K * tokens):
            row_copy(0, 0, sl).wait()

    @pl.when(i >= 2)
    def _():
        drain(slot)

    for gi, h_ref in enumerate(h_refs):
        @pl.when((i >= tile_starts[gi]) & (i < tile_starts[gi + 1]))
        def _(h_ref=h_ref):
            stage[pl.ds(pl.multiple_of(slot * blk, blk), blk), :] = h_ref[...]

    def body(t, carry):
        for k in range(TOP_K):
            row_copy(k, t, slot).start()
        return carry

    lax.fori_loop(0, tokens, body, 0, unroll=DMA_UNROLL // TOP_K)

    @pl.when(i == n - 1)
    def _():
        @pl.when(n >= 2)
        def _():
            drain(1 - slot)
        drain(slot)


def _dispatch(hs, pos_tiles, n_rows):
    TT = TOKEN_TILE
    blk = TT * SUBLANES
    tile_starts = [0]
    for h in hs:
        assert h.shape[0] % blk == 0
        tile_starts.append(tile_starts[-1] + h.shape[0] // blk)
    specs = [pl.BlockSpec((1, SUBLANES, TT), lambda i: (i, 0, 0), memory_space=pltpu.SMEM)]
    for gi, h in enumerate(hs):
        lo, n = tile_starts[gi], h.shape[0] // blk
        specs.append(pl.BlockSpec((blk, LANES), lambda i, lo=lo, n=n: (jnp.clip(i - lo, 0, n - 1), 0)))
    return pl.pallas_call(
        functools.partial(_dispatch_kernel, tokens=TT, tile_starts=tuple(tile_starts)),
        out_shape=jax.ShapeDtypeStruct((n_rows * SUBLANES, LANES), F32),
        grid=(tile_starts[-1],),
        in_specs=specs,
        out_specs=pl.BlockSpec(memory_space=pl.ANY),
        scratch_shapes=[pltpu.VMEM((2 * blk, LANES), F32), pltpu.SemaphoreType.DMA((2,))],
        compiler_params=_params("arbitrary"),
        name="dispatch",
    )(pos_tiles, *hs)


def _expert_kernel(ib_ref, ie_ref, lo_ref, hi_ref, x_ref, wgu_ref, bgu_ref, wd_ref, bd_ref, y_ref, wgu_s, wd_s):
    i = pl.program_id(0)
    prev = jnp.maximum(i - 1, 0)
    lo, hi = lo_ref[i], hi_ref[i]
    first = (i == 0) | (ib_ref[i] != ib_ref[prev])

    @pl.when(hi > lo)
    def _():
        @pl.when((i == 0) | (ie_ref[i] != ie_ref[prev]))
        def _():
            wgu_s[...] = wgu_ref[...].astype(BF16)
            wd_s[...] = wd_ref[...].astype(BF16)

        F = wd_s.shape[0]
        rows = x_ref.shape[0] // SUBLANES
        x = _load_row_tiles(x_ref, 0, rows).astype(BF16)
        gu = jnp.dot(x, wgu_s[...], preferred_element_type=F32) + bgu_ref[...]
        gl = jnp.minimum(gu[:, :F], SWIGLU_LIMIT)
        up = jnp.clip(gu[:, F:], -SWIGLU_LIMIT, SWIGLU_LIMIT)
        act = gl * _sigmoid(SWIGLU_ALPHA * gl) * (up + 1.0)
        y = jnp.dot(act.astype(BF16), wd_s[...], preferred_element_type=F32) + bd_ref[...]
        rid = lax.broadcasted_iota(I32, (rows, 1), 0)
        mine = (rid >= lo) & (rid < hi)

        @pl.when(first)
        def _():
            _store_row_tiles(y_ref, 0, jnp.where(mine, y, 0.0))

        @pl.when(jnp.logical_not(first))
        def _():
            _store_row_tiles(y_ref, 0, jnp.where(mine, y, _load_row_tiles(y_ref, 0, rows)))


def _experts(x_sorted, items, w_gate_up, b_gate_up, w_down, b_down, layer):
    L, E, D, F2 = w_gate_up.shape
    F = w_down.shape[2]
    assert D == SUBLANES * LANES
    rows = MOE_ROWS
    blk = rows * SUBLANES
    n_items = items[0].shape[0]
    grid_spec = pltpu.PrefetchScalarGridSpec(
        num_scalar_prefetch=4,
        grid=(n_items,),
        in_specs=[
            pl.BlockSpec((blk, LANES), lambda i, ib, ie, lo, hi: (ib[i], 0)),
            pl.BlockSpec((None, None, D, F2), lambda i, ib, ie, lo, hi: (layer, ie[i], 0, 0)),
            pl.BlockSpec((None, None, 1, F2), lambda i, ib, ie, lo, hi: (layer, ie[i], 0, 0)),
            pl.BlockSpec((None, None, F, D), lambda i, ib, ie, lo, hi: (layer, ie[i], 0, 0)),
            pl.BlockSpec((None, None, 1, D), lambda i, ib, ie, lo, hi: (layer, ie[i], 0, 0)),
        ],
        out_specs=pl.BlockSpec((blk, LANES), lambda i, ib, ie, lo, hi: (ib[i], 0)),
        scratch_shapes=[pltpu.VMEM((D, F2), BF16), pltpu.VMEM((F, D), BF16)],
    )
    return pl.pallas_call(
        _expert_kernel,
        out_shape=jax.ShapeDtypeStruct(x_sorted.shape, F32),
        grid_spec=grid_spec,
        compiler_params=_params("arbitrary"),
        name="experts",
    )(*items, x_sorted, w_gate_up, b_gate_up.reshape(L, E, 1, F2), w_down, b_down.reshape(L, E, 1, D))


def _combine_kernel(pos0_ref, pos1_ref, y_hbm, gate_ref, x_ref, gt_ref, g_ref, o_ref, ybuf, sem, *, tokens):
    TT = tokens
    n_rows = TOP_K * TT
    i = pl.program_id(0) * pl.num_programs(1) + pl.program_id(1)
    n = pl.num_programs(0) * pl.num_programs(1)
    slot = lax.rem(i, 2)

    def row_copy(pos_ref, k, t, sl):
        return pltpu.make_async_copy(_row_tile(y_hbm, pos_ref[0, k, t]), _row_tile(ybuf, sl * n_rows + k * TT + t),
                                     sem.at[sl])

    def start_rows(pos_ref, sl):
        def body(t, carry):
            for k in range(TOP_K):
                row_copy(pos_ref, k, t, sl).start()
            return carry
        lax.fori_loop(0, TT, body, 0, unroll=DMA_UNROLL // TOP_K)

    @pl.when(i == 0)
    def _():
        start_rows(pos0_ref, 0)

    @pl.when(i + 1 < n)
    def _():
        start_rows(pos1_ref, 1 - slot)

    for _ in range(n_rows):
        row_copy(pos0_ref, 0, 0, slot).wait()

    gates = gate_ref[0]
    f = gates[:, 0:1] * _load_row_tiles(ybuf, slot * n_rows, TT)
    for k in range(1, TOP_K):
        f = f + gates[:, k:k + 1] * _load_row_tiles(ybuf, slot * n_rows + k * TT, TT)
    o_ref[0] = x_ref[0] + gt_ref[0] * _rms(f, g_ref[...])


def _combine(y_sorted, pos_tiles, gates3, x3, gt, g_post, per_row):
    G, R, D = x3.shape
    TT = TOKEN_TILE
    assert R % TT == 0
    nr = R // TT
    n_tiles = G * nr
    row = lambda w: pl.BlockSpec((1, TT, w), lambda g_, r: (g_, r, 0))
    cur = pl.BlockSpec((1, SUBLANES, TT), lambda g_, r: (g_ * nr + r, 0, 0), memory_space=pltpu.SMEM)
    nxt = pl.BlockSpec((1, SUBLANES, TT), lambda g_, r: (jnp.minimum(g_ * nr + r + 1, n_tiles - 1), 0, 0),
                       memory_space=pltpu.SMEM)
    return pl.pallas_call(
        functools.partial(_combine_kernel, tokens=TT),
        out_shape=jax.ShapeDtypeStruct((G, R, D), F32),
        grid=(G, nr),
        in_specs=[cur, nxt, pl.BlockSpec(memory_space=pl.ANY),
                  row(LANES), row(D), _mod_spec(per_row, TT, D), pl.BlockSpec((1, D), lambda g_, r: (0, 0))],
        out_specs=row(D),
        scratch_shapes=[pltpu.VMEM((2 * TOP_K * TT * SUBLANES, LANES), F32), pltpu.SemaphoreType.DMA((2,))],
        compiler_params=_params("arbitrary", "arbitrary"),
        name="combine",
    )(pos_tiles, pos_tiles, y_sorted, gates3, x3, gt, g_post.reshape(1, D))


def _dest_kernel(idx_ref, rank_ref, start_ref, o_ref):
    idx, rank = idx_ref[0], rank_ref[0].astype(F32)
    start = start_ref[...].astype(F32)
    lane = lax.broadcasted_iota(I32, idx.shape, 1)
    dest = jnp.zeros(idx.shape, F32)
    for k in range(TOP_K):
        first = jnp.sum(jnp.where(lane == idx[:, k:k + 1], start, 0.0), axis=1, keepdims=True)
        dest = jnp.where(lane == k, first + rank[:, k:k + 1], dest)
    o_ref[0] = dest.T[0:SUBLANES, :].astype(I32)


def _dest(idx3, rank3, start_vec):
    G, R, _ = idx3.shape
    TT = TOKEN_TILE
    assert TT == LANES and R % TT == 0 and TOP_K <= SUBLANES
    nr = R // TT
    blk = pl.BlockSpec((1, TT, LANES), lambda g_, r: (g_, r, 0))
    return pl.pallas_call(
        _dest_kernel,
        out_shape=jax.ShapeDtypeStruct((G * nr, SUBLANES, TT), I32),
        grid=(G, nr),
        in_specs=[blk, blk, pl.BlockSpec((1, LANES), lambda g_, r: (0, 0))],
        out_specs=pl.BlockSpec((1, SUBLANES, TT), lambda g_, r: (g_ * nr + r, 0, 0)),
        compiler_params=_params("parallel", "parallel"),
        name="dest",
    )(idx3, rank3, start_vec)


def _prefix_sums(v):
    n = v.shape[0]
    keep = jnp.arange(n, dtype=I32)[:, None] >= jnp.arange(n, dtype=I32)[None, :]
    return jnp.sum(jnp.where(keep, v[None, :], 0), axis=1)


def _work_items(counts, n_rows):
    E = counts.shape[0]
    rows = MOE_ROWS
    n_blocks = n_rows // rows
    n_items = n_blocks + E - 1
    end = _prefix_sums(counts)
    start = end - counts
    first_blk = start // rows
    last_blk = jnp.maximum(end - 1, 0) // rows
    per_e = jnp.where(counts > 0, last_blk - first_blk + 1, 0)
    item_end = _prefix_sums(per_e)
    item_start = item_end - per_e
    n_real = item_end[-1]
    i = jnp.clip(jnp.arange(n_items, dtype=I32), 0, jnp.maximum(n_real - 1, 0))
    e = jnp.minimum(jnp.sum((item_end[None, :] <= i[:, None]).astype(I32), axis=1), E - 1)
    b = (first_blk[e] + i - item_start[e]).astype(I32)
    lo = jnp.maximum(start[e], b * rows) - b * rows
    hi = jnp.minimum(end[e], (b + 1) * rows) - b * rows
    hi = jnp.where(jnp.arange(n_items, dtype=I32) < n_real, hi, lo)
    return b, e, lo.astype(I32), hi.astype(I32)


def _moe(groups, p, l):
    E = p["w_router"].shape[2]
    D = groups[0]["x3"].shape[2]
    n_rows = TOP_K * sum(g["x3"].shape[0] * g["x3"].shape[1] for g in groups)
    assert n_rows % MOE_ROWS == 0
    counts = jnp.zeros((1, LANES), I32)
    routed = []
    for g in groups:
        h3, idx3, gates3, rank3, counts = _router(g["x3"], p["g_pre_ffn"][l], g["sc2"], g["sh2"],
                                                  p["w_router"][l], p["b_router"][l], counts, g["per_row"])
        routed.append((h3, idx3, gates3, rank3))
    counts = counts[0, :E]
    start = _prefix_sums(counts) - counts
    start_vec = jnp.pad(start, (0, LANES - E)).reshape(1, LANES)
    tiles = [_dest(idx3, rank3, start_vec) for _, idx3, _, rank3 in routed]
    x_sorted = _dispatch([r[0] for r in routed], jnp.concatenate(tiles, axis=0), n_rows)
    y_sorted = _experts(x_sorted, _work_items(counts, n_rows), p["w_gate_up"], p["b_gate_up"], p["w_down"],
                        p["b_down"], l)
    return [_combine(y_sorted, t, r[2], g["x3"], g["gt2"], p["g_post_ffn"][l], g["per_row"])
            for g, r, t in zip(groups, routed, tiles)]


def _mix(grp, p, l, w_tail, w_ba):
    x3, c, per_row = grp["x3"], grp["c"], grp["per_row"]
    B, T = grp["B"], grp["T"]
    G, R, D = x3.shape
    N = G * R
    H = p["a_log"].shape[1]
    KW = H * HEAD_DIM
    QKV = 3 * KW
    mod = (lambda a: a.reshape(1, B, D)) if per_row else (lambda a: a.reshape(B, 1, D))
    ada = _mm(c, p["w_ada"], F32, layer=l, n=6 * D, b=p["b_ada"], in_silu=True, name="ada")
    sh1, sc1, gt1, sh2, sc2, gt2 = (mod(ada[:, i * D:(i + 1) * D]) for i in range(6))
    h = _norm_mod(x3, p["g_pre_mix"][l], sc1, sh1, per_row).reshape(N, D)
    qkv = _mm(h, p["w_in"], F32, layer=l, n=QKV, name="proj_qkv")
    zs = _mm(h, p["w_in"], F32, layer=l, n=KW, col=QKV, act="silu", name="proj_z")
    ba = _mm(h, w_ba, F32, layer=l, n=LANES, name="proj_ba")
    u = _mm(h, w_tail, F32, layer=l, n=D, col=0, col2=D, name="proj_glu")
    mg = _mm(h, w_tail, F32, layer=l, n=2 * D, col=2 * D, act="sigmoid", name="proj_gate")
    alog_vec = jnp.pad(p["a_log"][l], (H, LANES - 2 * H)).reshape(1, LANES)
    dt_vec = jnp.pad(p["dt_bias"][l], (H, LANES - 2 * H)).reshape(1, LANES)
    conv_w = (p["w_dw_conv"][l], p["b_dw_conv"][l], p["g_ln_conv"][l], p["b_ln_conv"][l])
    if per_row:
        st_q = jnp.swapaxes(grp["buf_qkv"][l], 0, 1)
        dn, nst_q, s_new = _delta_step(qkv, ba, zs, st_q, grp["s_dn"][l], p["w_qkv_conv"][l], alog_vec, dt_vec,
                                       p["g_dn_norm"][l], H)
        bq = jnp.swapaxes(nst_q, 0, 1)
        cv, nst_g = _glu_conv_step(u, jnp.swapaxes(grp["buf_glu"][l], 0, 1), *conv_w)
        bg = jnp.swapaxes(nst_g, 0, 1)
    else:
        dn, s_new = _delta_prompt(qkv.reshape(B, T, QKV), ba.reshape(B, T, LANES), zs.reshape(B, T, KW),
                                  p["w_qkv_conv"][l], alog_vec, dt_vec, p["g_dn_norm"][l], H)
        bq = qkv.reshape(B, T, QKV)[:, T - (p["w_qkv_conv"].shape[1] - 1):, :]
        cv = _glu_conv_prompt(u.reshape(B, T, D), *conv_w)
        bg = u.reshape(B, T, D)[:, T - (p["w_dw_conv"].shape[1] - 1):, :]
    x3 = _merge_out(cv.reshape(G, R, D), dn.reshape(G, R, KW), mg.reshape(G, R, 2 * D), x3, gt1,
                    p["w_pw_conv"], p["b_pw_conv"], p["w_out"], p["g_post_mix"], per_row, l)
    return dict(grp, x3=x3, sc2=sc2, sh2=sh2, gt2=gt2), (s_new, bq, bg)


def kernel(x_prompt, x_sample, c_prompt, c_sample, state_dn, state_qkv_conv, state_glu_conv, w_ada, b_ada, g_pre_mix, g_post_mix, g_pre_ffn, g_post_ffn, w_in, w_qkv_conv, a_log, dt_bias, g_dn_norm, w_dw_conv, b_dw_conv, g_ln_conv, b_ln_conv, w_pw_conv, b_pw_conv, w_out, w_router, b_router, w_gate_up, b_gate_up, w_down, b_down):
    p = dict(w_ada=w_ada, b_ada=b_ada, g_pre_mix=g_pre_mix, g_post_mix=g_post_mix, g_pre_ffn=g_pre_ffn,
             g_post_ffn=g_post_ffn, w_in=w_in, w_qkv_conv=w_qkv_conv, a_log=a_log, dt_bias=dt_bias,
             g_dn_norm=g_dn_norm, w_dw_conv=w_dw_conv, b_dw_conv=b_dw_conv, g_ln_conv=g_ln_conv,
             b_ln_conv=b_ln_conv, w_pw_conv=w_pw_conv, b_pw_conv=b_pw_conv, w_out=w_out,
             w_router=w_router, b_router=b_router, w_gate_up=w_gate_up, b_gate_up=b_gate_up,
             w_down=w_down, b_down=b_down)
    depth, D = w_in.shape[0], w_in.shape[1]
    H = a_log.shape[1]
    o_ba = 4 * H * HEAD_DIM
    w_ba = jnp.pad(w_in[:, :, o_ba:o_ba + 2 * H], ((0, 0), (0, 0), (0, LANES - 2 * H)))
    w_tail = w_in[:, :, o_ba + 2 * H:]
    Bp, Tp, _ = x_prompt.shape
    Bs, Ts, _ = x_sample.shape
    assert Tp > 1 and Ts == 1
    groups = [dict(x3=x_prompt, c=c_prompt, per_row=False, B=Bp, T=Tp),
              dict(x3=x_sample.reshape(1, Bs, D), c=c_sample, per_row=True, B=Bs, T=Ts,
                   s_dn=state_dn, buf_qkv=state_qkv_conv, buf_glu=state_glu_conv)]
    states = [[], []]
    for l in range(depth):
        mixed = []
        for gi, grp in enumerate(groups):
            grp, st = _mix(grp, p, l, w_tail, w_ba)
            mixed.append(grp)
            states[gi].append(st)
        outs = _moe(mixed, p, l)
        groups = [dict(grp, x3=x3) for grp, x3 in zip(mixed, outs)]
    stack = lambda gi, j, like: jnp.stack([st[j] for st in states[gi]]).astype(like.dtype)
    return (groups[0]["x3"], groups[1]["x3"].reshape(Bs, Ts, D),
            stack(0, 0, state_dn), stack(0, 1, state_qkv_conv), stack(0, 2, state_glu_conv),
            stack(1, 0, state_dn), stack(1, 1, state_qkv_conv), stack(1, 2, state_glu_conv))
```
